```python
import jax
import jax.numpy as jnp
from jax import lax
import numpy as np

D_MODEL = 1024
BATCH = 1
SEQ = 16384
DEPTH = 2
DEC_BATCH = 128
DEC_SEQ = 4
PAST_LEN = 16384
PAGE_SIZE = 128

N_A_LAYERS = DEPTH // 2
N_B_LAYERS = DEPTH - N_A_LAYERS
MLA_HEADS = 8
Q_LORA = 384
KV_LORA = 256
QK_NOPE = 128
QK_ROPE = 64
V_DIM = 128
MLA_SCALE = (QK_NOPE + QK_ROPE) ** -0.5
SWA_HEADS = 16
SWA_KV_HEADS = 4
SWA_HEAD_DIM = 64
WINDOW = 128
SWA_SCALE = SWA_HEAD_DIM ** -0.5
D_FF = 2816
FFN_RES_WEIGHT = 0.5
ROPE_THETA = 10000.0
EPS = 1e-6
Q_BLOCK = 128

kernel_name = 'yoco_mla_swa_sink_macaron_decode_step'


def rmsnorm(x, g):
    xf = x.astype(jnp.float32)
    y = xf * lax.rsqrt(jnp.mean(xf * xf, axis=-1, keepdims=True) + EPS)
    return (y * g.astype(jnp.float32)).astype(x.dtype)


def rope(x, pos):
    half = x.shape[-1] // 2
    inv_freq = jnp.exp(-(jnp.arange(half, dtype=jnp.float32) / half) * jnp.log(jnp.float32(ROPE_THETA)))
    ang = pos.astype(jnp.float32)[:, None] * inv_freq[None, :]
    shp = (pos.shape[0],) + (1,) * (x.ndim - 3) + (half,)
    cos = jnp.cos(ang).reshape(shp)
    sin = jnp.sin(ang).reshape(shp)
    xf = x.astype(jnp.float32)
    x1, x2 = xf[..., :half], xf[..., half:]
    return jnp.concatenate([x1 * cos - x2 * sin, x2 * cos + x1 * sin], axis=-1).astype(x.dtype)


def swiglu_half(x, g, w_in, w_out):
    gate, up = jnp.split(rmsnorm(x, g) @ w_in, 2, axis=-1)
    return x + FFN_RES_WEIGHT * ((jax.nn.silu(gate) * up) @ w_out)


def sink_softmax(s, sink):
    m = jnp.maximum(jnp.max(s, axis=-1, keepdims=True), sink)
    p = jnp.exp(s - m)
    return p / (jnp.sum(p, axis=-1, keepdims=True) + jnp.exp(sink - m))


def mla_project(h, pos, p, l):
    b, s, _ = h.shape
    cq = rmsnorm(h @ p['mla_w_dq'][l], p['mla_q_norm'][l])
    q = (cq @ p['mla_w_uq'][l]).reshape(b, s, MLA_HEADS, QK_NOPE + QK_ROPE)
    q_lat = jnp.einsum('bshn,chn->bshc', q[..., :QK_NOPE], p['mla_w_uk'][l])
    q_pe = rope(q[..., QK_NOPE:], pos)
    ckv = h @ p['mla_w_dkv'][l]
    c = rmsnorm(ckv[..., :KV_LORA], p['mla_kv_norm'][l])
    k_pe = rope(ckv[..., KV_LORA:], pos)
    return jnp.concatenate([q_lat, q_pe], axis=-1), jnp.concatenate([c, k_pe], axis=-1)


def mla_attend_prompt(q, rows):
    b, s, h, e = q.shape
    nb = s // Q_BLOCK
    qb = q.reshape(b, nb, Q_BLOCK, h, e).transpose(1, 0, 2, 3, 4)
    kpos = jnp.arange(s)
    lat = rows[..., :KV_LORA]

    def block(args):
        i, qi = args
        sc = jnp.einsum('bqhe,bke->bhqk', qi, rows, preferred_element_type=jnp.float32) * MLA_SCALE
        qpos = i * Q_BLOCK + jnp.arange(Q_BLOCK)
        sc = jnp.where(kpos[None, :] <= qpos[:, None], sc, -jnp.inf)
        pr = jax.nn.softmax(sc, axis=-1).astype(rows.dtype)
        return jnp.einsum('bhqk,bkc->bqhc', pr, lat)

    o = lax.map(block, (jnp.arange(nb), qb))
    return o.transpose(1, 0, 2, 3, 4).reshape(b, s, h, KV_LORA)


def mla_attend_sample(q, rows, cache_l, page_table):
    e = cache_l.shape[-1]

    def one(args):
        qi, ri, pt = args
        past = cache_l[pt].reshape(-1, e)
        keys = jnp.concatenate([past, ri.astype(past.dtype)], axis=0)
        n_past = past.shape[0]
        ds = qi.shape[0]
        sc = jnp.einsum('qhe,ke->hqk', qi, keys, preferred_element_type=jnp.float32) * MLA_SCALE
        mask = jnp.arange(n_past + ds)[None, :] <= (n_past + jnp.arange(ds))[:, None]
        sc = jnp.where(mask, sc, -jnp.inf)
        pr = jax.nn.softmax(sc, axis=-1).astype(keys.dtype)
        return jnp.einsum('hqk,kc->qhc', pr, keys[:, :KV_LORA])

    return lax.map(one, (q, rows, page_table))


def swa_attend_prompt(q, k, v, sinks):
    b, s, h, d = q.shape
    g = h // SWA_KV_HEADS
    nb = s // WINDOW
    qb = q.reshape(b, nb, WINDOW, SWA_KV_HEADS, g, d)

    def band(t):
        tb = t.reshape(b, nb, WINDOW, SWA_KV_HEADS, d)
        prev = jnp.concatenate([jnp.zeros_like(tb[:, :1]), tb[:, :-1]], axis=1)
        return jnp.concatenate([prev, tb], axis=2)

    kk, vv = band(k), band(v)
    sc = jnp.einsum('bnqkgd,bnjkd->bnkgqj', qb, kk, preferred_element_type=jnp.float32) * SWA_SCALE
    rel = (WINDOW + jnp.arange(WINDOW))[:, None] - jnp.arange(2 * WINDOW)[None, :]
    kabs = (jnp.arange(nb)[:, None, None] - 1) * WINDOW + jnp.arange(2 * WINDOW)[None, None, :]
    mask = (rel >= 0) & (rel < WINDOW) & (kabs >= 0)
    sc = jnp.where(mask[None, :, None, None], sc, -jnp.inf)
    sk = sinks.astype(jnp.float32).reshape(SWA_KV_HEADS, g)[None, None, :, :, None, None]
    pr = sink_softmax(sc, sk)
    o = jnp.einsum('bnkgqj,bnjkd->bnqkgd', pr.astype(vv.dtype), vv)
    return o.reshape(b, s, h, d)


def swa_attend_sample(q, k_new, v_new, sinks, k_buf, v_buf):
    b, s, h, d = q.shape
    g = h // SWA_KV_HEADS
    wb = k_buf.shape[1]
    kk = jnp.concatenate([k_buf, k_new.astype(k_buf.dtype)], axis=1)
    vv = jnp.concatenate([v_buf, v_new.astype(v_buf.dtype)], axis=1)
    qg = q.reshape(b, s, SWA_KV_HEADS, g, d)
    sc = jnp.einsum('bqkgd,bjkd->bkgqj', qg, kk, preferred_element_type=jnp.float32) * SWA_SCALE
    rel = (wb + jnp.arange(s))[:, None] - jnp.arange(wb + s)[None, :]
    mask = (rel >= 0) & (rel < WINDOW)
    sc = jnp.where(mask, sc, -jnp.inf)
    sk = sinks.astype(jnp.float32).reshape(SWA_KV_HEADS, g)[None, :, :, None, None]
    pr = sink_softmax(sc, sk)
    o = jnp.einsum('bkgqj,bjkd->bqkgd', pr.astype(vv.dtype), vv)
    return o.reshape(b, s, h, d)


def run_trunk(x, pos, p, mla_attend, swa_attend):
    b, s, _ = x.shape
    rows_all = []
    k = None
    v = None
    for l in range(DEPTH):
        if l == N_A_LAYERS:
            hk = rmsnorm(x, p['kv_norm'])
            kv = (hk @ p['swa_w_kv'] + p['swa_b_kv']).reshape(b, s, 2, SWA_KV_HEADS, SWA_HEAD_DIM)
            k = rope(kv[:, :, 0], pos)
            v = kv[:, :, 1]
        x = swiglu_half(x, p['ffn_norm'][l, 0], p['ffn_w_in'][l, 0], p['ffn_w_out'][l, 0])
        if l < N_A_LAYERS:
            h = rmsnorm(x, p['mla_norm'][l])
            q, rows = mla_project(h, pos, p, l)
            o_lat = mla_attend(l, q, rows)
            o = jnp.einsum('bshc,chv->bshv', o_lat, p['mla_w_uv'][l]).reshape(b, s, MLA_HEADS * V_DIM)
            x = x + o @ p['mla_w_o'][l]
            rows_all.append(rows)
        else:
            j = l - N_A_LAYERS
            h = rmsnorm(x, p['swa_norm'][j])
            q = rope((h @ p['swa_w_q'][j] + p['swa_b_q'][j]).reshape(b, s, SWA_HEADS, SWA_HEAD_DIM), pos)
            o = swa_attend(q, k, v, p['swa_sinks'][j])
            x = x + o.reshape(b, s, SWA_HEADS * SWA_HEAD_DIM) @ p['swa_w_o'][j] + p['swa_b_o'][j]
        x = swiglu_half(x, p['ffn_norm'][l, 1], p['ffn_w_in'][l, 1], p['ffn_w_out'][l, 1])
    return rmsnorm(x, p['final_norm']), jnp.stack(rows_all, axis=0), k, v


def setup_inputs(seed: int = 0) -> dict:
    key = jax.random.key(seed)
    ks = iter(jax.random.split(key, 40))
    n_pages = PAST_LEN // PAGE_SIZE
    n_used = DEC_BATCH * n_pages
    n_phys = n_used + n_used // 4
    w_buf = min(WINDOW, PAST_LEN)
    e = KV_LORA + QK_ROPE

    def nrm(shape, fan_in):
        return jax.random.normal(next(ks), shape, jnp.float32) * (fan_in ** -0.5)

    def gain(shape):
        return 1.0 + 0.05 * jax.random.normal(next(ks), shape, jnp.float32)

    def bias(shape):
        return 0.02 * jax.random.normal(next(ks), shape, jnp.float32)

    def act(shape):
        return jax.random.normal(next(ks), shape, jnp.float32)

    page_table = jax.random.permutation(next(ks), n_phys)[:n_used].reshape(DEC_BATCH, n_pages).astype(jnp.int32)
    return {
        'x_prompt': act((BATCH, SEQ, D_MODEL)),
        'x_sample': act((DEC_BATCH, DEC_SEQ, D_MODEL)),
        'cache_mla': act((N_A_LAYERS, n_phys, PAGE_SIZE, e)),
        'cache_swa_k': act((DEC_BATCH, w_buf, SWA_KV_HEADS, SWA_HEAD_DIM)),
        'cache_swa_v': act((DEC_BATCH, w_buf, SWA_KV_HEADS, SWA_HEAD_DIM)),
        'page_table': page_table,
        'ffn_norm': gain((DEPTH, 2, D_MODEL)),
        'ffn_w_in': nrm((DEPTH, 2, D_MODEL, 2 * D_FF), D_MODEL),
        'ffn_w_out': nrm((DEPTH, 2, D_FF, D_MODEL), D_FF),
        'mla_norm': gain((N_A_LAYERS, D_MODEL)),
        'mla_w_dq': nrm((N_A_LAYERS, D_MODEL, Q_LORA), D_MODEL),
        'mla_q_norm': gain((N_A_LAYERS, Q_LORA)),
        'mla_w_uq': nrm((N_A_LAYERS, Q_LORA, MLA_HEADS * (QK_NOPE + QK_ROPE)), Q_LORA),
        'mla_w_dkv': nrm((N_A_LAYERS, D_MODEL, e), D_MODEL),
        'mla_kv_norm': gain((N_A_LAYERS, KV_LORA)),
        'mla_w_uk': nrm((N_A_LAYERS, KV_LORA, MLA_HEADS, QK_NOPE), KV_LORA),
        'mla_w_uv': nrm((N_A_LAYERS, KV_LORA, MLA_HEADS, V_DIM), KV_LORA),
        'mla_w_o': nrm((N_A_LAYERS, MLA_HEADS * V_DIM, D_MODEL), MLA_HEADS * V_DIM),
        'kv_norm': gain((D_MODEL,)),
        'swa_w_kv': nrm((D_MODEL, 2 * SWA_KV_HEADS * SWA_HEAD_DIM), D_MODEL),
        'swa_b_kv': bias((2 * SWA_KV_HEADS * SWA_HEAD_DIM,)),
        'swa_norm': gain((N_B_LAYERS, D_MODEL)),
        'swa_w_q': nrm((N_B_LAYERS, D_MODEL, SWA_HEADS * SWA_HEAD_DIM), D_MODEL),
        'swa_b_q': bias((N_B_LAYERS, SWA_HEADS * SWA_HEAD_DIM)),
        'swa_sinks': 0.5 * jax.random.normal(next(ks), (N_B_LAYERS, SWA_HEADS), jnp.float32),
        'swa_w_o': nrm((N_B_LAYERS, SWA_HEADS * SWA_HEAD_DIM, D_MODEL), SWA_HEADS * SWA_HEAD_DIM),
        'swa_b_o': bias((N_B_LAYERS, D_MODEL)),
        'final_norm': gain((D_MODEL,)),
    }


def reference(x_prompt, x_sample, cache_mla, cache_swa_k, cache_swa_v, page_table,
              ffn_norm, ffn_w_in, ffn_w_out,
              mla_norm, mla_w_dq, mla_q_norm, mla_w_uq, mla_w_dkv, mla_kv_norm, mla_w_uk, mla_w_uv, mla_w_o,
              kv_norm, swa_w_kv, swa_b_kv,
              swa_norm, swa_w_q, swa_b_q, swa_sinks, swa_w_o, swa_b_o,
              final_norm):
    p = {
        'ffn_norm': ffn_norm, 'ffn_w_in': ffn_w_in, 'ffn_w_out': ffn_w_out,
        'mla_norm': mla_norm, 'mla_w_dq': mla_w_dq, 'mla_q_norm': mla_q_norm, 'mla_w_uq': mla_w_uq,
        'mla_w_dkv': mla_w_dkv, 'mla_kv_norm': mla_kv_norm, 'mla_w_uk': mla_w_uk, 'mla_w_uv': mla_w_uv,
        'mla_w_o': mla_w_o,
        'kv_norm': kv_norm, 'swa_w_kv': swa_w_kv, 'swa_b_kv': swa_b_kv,
        'swa_norm': swa_norm, 'swa_w_q': swa_w_q, 'swa_b_q': swa_b_q, 'swa_sinks': swa_sinks,
        'swa_w_o': swa_w_o, 'swa_b_o': swa_b_o,
        'final_norm': final_norm,
    }
    seq = x_prompt.shape[1]
    dseq = x_sample.shape[1]
    past_len = page_table.shape[1] * cache_mla.shape[2]
    pos_prompt = jnp.arange(seq, dtype=jnp.int32)
    pos_sample = past_len + jnp.arange(dseq, dtype=jnp.int32)

    y_prompt, mla_rows_prompt, k_p, v_p = run_trunk(
        x_prompt, pos_prompt, p,
        lambda l, q, rows: mla_attend_prompt(q, rows),
        swa_attend_prompt)
    y_sample, mla_rows_sample, k_s, v_s = run_trunk(
        x_sample, pos_sample, p,
        lambda l, q, rows: mla_attend_sample(q, rows, cache_mla[l], page_table),
        lambda q, k, v, sk: swa_attend_sample(q, k, v, sk, cache_swa_k, cache_swa_v))

    w_p = min(WINDOW, seq)
    swa_k_prompt = k_p[:, seq - w_p:]
    swa_v_prompt = v_p[:, seq - w_p:]
    swa_k_sample = jnp.concatenate([cache_swa_k, k_s.astype(cache_swa_k.dtype)], axis=1)[:, dseq:]
    swa_v_sample = jnp.concatenate([cache_swa_v, v_s.astype(cache_swa_v.dtype)], axis=1)[:, dseq:]
    return (y_prompt, y_sample, mla_rows_prompt, mla_rows_sample,
            swa_k_prompt, swa_v_prompt, swa_k_sample, swa_v_sample)
```

```python
import functools

import jax
import jax.numpy as jnp
from jax import lax
from jax.experimental import pallas as pl
from jax.experimental.pallas import tpu as pltpu

F32 = jnp.float32
BF16 = jnp.bfloat16

EPS = 1e-6
ROPE_THETA = 10000.0
FFN_RES_WEIGHT = 0.5
LANES = 128
ROPE_DIM = 64
Q_BLOCK = 128
VMEM_LIMIT = 56 * 1024 * 1024
NEG_INF = float("-inf")


def _cparams(n_axes, vmem=None):
    return pltpu.CompilerParams(dimension_semantics=("arbitrary",) * n_axes, vmem_limit_bytes=vmem)


def _const_spec(shape):
    nd = len(shape)
    return pl.BlockSpec(shape, lambda *_: (0,) * nd, pipeline_mode=pl.Buffered(1))


def _dot(a, b):
    return jnp.dot(a, b, preferred_element_type=F32)


def _dot_nt(a, b):
    return lax.dot_general(a, b, (((1,), (1,)), ((), ())), preferred_element_type=F32)


def _idiv(x, n):
    if n & (n - 1) == 0:
        return x >> (n.bit_length() - 1)
    return lax.div(x, jnp.full(x.shape, n, x.dtype))


def _irem(x, n):
    if n & (n - 1) == 0:
        return x & (n - 1)
    return lax.rem(x, jnp.full(x.shape, n, x.dtype))


def _rms(x, g):
    return x * lax.rsqrt(jnp.mean(x * x, axis=-1, keepdims=True) + EPS) * g


def _rope128(x, cos_t, sin_t):
    lane = lax.broadcasted_iota(jnp.int32, x.shape, 1)
    first_half = (lane & (ROPE_DIM // 2)) == 0
    partner = jnp.where(first_half, pltpu.roll(x, LANES - ROPE_DIM // 2, 1), pltpu.roll(x, ROPE_DIM // 2, 1))
    return x * cos_t + partner * sin_t


def _rope_tables(pos):
    half = ROPE_DIM // 2
    inv_freq = jnp.exp(-(jnp.arange(half, dtype=F32) / half) * jnp.log(F32(ROPE_THETA)))
    ang = pos.astype(F32)[:, None] * inv_freq[None, :]
    cos, sin = jnp.cos(ang), jnp.sin(ang)
    reps = LANES // ROPE_DIM
    cos_t = jnp.tile(cos, (1, 2 * reps))
    sin_t = jnp.tile(jnp.concatenate([-sin, sin], axis=1), (1, reps))
    return cos_t, sin_t


def _ffn_kernel(x_ref, g_ref, win_ref, wout_ref, *rest, d_ff, n_chunks, final):
    if final:
        gf_ref, o_ref = rest
    else:
        (o_ref,) = rest
    x = x_ref[...]
    h = _rms(x, g_ref[...]).astype(BF16)
    ck = d_ff // n_chunks
    acc = None
    for c in range(n_chunks):
        gate = _dot(h, win_ref[:, c * ck:(c + 1) * ck])
        up = _dot(h, win_ref[:, d_ff + c * ck:d_ff + (c + 1) * ck])
        act = (gate * (1.0 / (1.0 + jnp.exp(-gate))) * up).astype(BF16)
        part = _dot(act, wout_ref[c * ck:(c + 1) * ck, :])
        acc = part if acc is None else acc + part
    y = x + FFN_RES_WEIGHT * acc
    if final:
        y = _rms(y, gf_ref[...])
    o_ref[...] = y


def _ffn(x, g, w_in, w_out, final_g=None, tm=512):
    m, d = x.shape
    d_ff = w_out.shape[0]
    tm = min(tm, m)
    n_chunks = 2 if d_ff % (2 * LANES) == 0 else 1
    in_specs = [
        pl.BlockSpec((tm, d), lambda i: (i, 0)),
        _const_spec((1, d)),
        _const_spec((d, 2 * d_ff)),
        _const_spec((d_ff, d)),
    ]
    args = [x, g.reshape(1, d), w_in, w_out]
    if final_g is not None:
        in_specs.append(_const_spec((1, d)))
        args.append(final_g.reshape(1, d))
    return pl.pallas_call(
        functools.partial(_ffn_kernel, d_ff=d_ff, n_chunks=n_chunks, final=final_g is not None),
        grid=(m // tm,),
        in_specs=in_specs,
        out_specs=pl.BlockSpec((tm, d), lambda i: (i, 0)),
        out_shape=jax.ShapeDtypeStruct((m, d), F32),
        compiler_params=_cparams(1, VMEM_LIMIT),
        name="ffn_half",
    )(*args)


def _mla_proj_kernel(x_ref, g_ref, wdq_ref, gq_ref, wuqn_ref, wuqr_ref, wuk_ref, wdkvc_ref, wdkvr_ref,
                     gkv_ref, cos_ref, sin_ref, q_ref, rows_ref, rowsb_ref, *, heads, nope, kv_lora):
    tm = x_ref.shape[0]
    cos_t, sin_t = cos_ref[...], sin_ref[...]
    h = _rms(x_ref[...], g_ref[...]).astype(BF16)
    cq = _rms(_dot(h, wdq_ref[...]), gq_ref[...]).astype(BF16)
    qn = _dot(cq, wuqn_ref[...])
    qr = _dot(cq, wuqr_ref[...])
    qr = jnp.concatenate(
        [_rope128(qr[:, c * LANES:(c + 1) * LANES], cos_t, sin_t) for c in range(qr.shape[1] // LANES)], axis=1)
    c_lat = _rms(_dot(h, wdkvc_ref[...]), gkv_ref[...])
    k_pe = _rope128(_dot(h, wdkvr_ref[...]), cos_t, sin_t)[:, :ROPE_DIM]
    rows_ref[:, :kv_lora] = c_lat
    rows_ref[:, kv_lora:] = k_pe
    rowsb_ref[:, :kv_lora] = c_lat.astype(BF16)
    rowsb_ref[:, kv_lora:] = k_pe.astype(BF16)
    for hd in range(heads):
        q_lat = _dot(qn[:, hd * nope:(hd + 1) * nope].astype(BF16), wuk_ref[hd]).astype(BF16)
        q_pe = qr[:, hd * ROPE_DIM:(hd + 1) * ROPE_DIM].astype(BF16)
        for r in range(tm // Q_BLOCK):
            q_ref[r, hd, :, :kv_lora] = q_lat[r * Q_BLOCK:(r + 1) * Q_BLOCK]
            q_ref[r, hd, :, kv_lora:] = q_pe[r * Q_BLOCK:(r + 1) * Q_BLOCK]


def _mla_proj(x, w, cos_t, sin_t, tm=256):
    m, d = x.shape
    heads, nope, kv_lora = w["w_uk"].shape
    e = kv_lora + ROPE_DIM
    q_lora = w["w_dq"].shape[1]
    tm = min(tm, m)
    nb = tm // Q_BLOCK
    consts = [w["norm"].reshape(1, d), w["w_dq"], w["q_norm"].reshape(1, q_lora), w["w_uq_n"], w["w_uq_r"],
              w["w_uk"], w["w_dkv_c"], w["w_dkv_r"], w["kv_norm"].reshape(1, kv_lora)]
    row_spec = lambda width: pl.BlockSpec((tm, width), lambda i: (i, 0))
    return pl.pallas_call(
        functools.partial(_mla_proj_kernel, heads=heads, nope=nope, kv_lora=kv_lora),
        grid=(m // tm,),
        in_specs=[row_spec(d)] + [_const_spec(c.shape) for c in consts] + [row_spec(LANES), row_spec(LANES)],
        out_specs=[pl.BlockSpec((nb, heads, Q_BLOCK, e), lambda i: (i, 0, 0, 0)), row_spec(e), row_spec(e)],
        out_shape=[jax.ShapeDtypeStruct((m // Q_BLOCK, heads, Q_BLOCK, e), BF16),
                   jax.ShapeDtypeStruct((m, e), F32),
                   jax.ShapeDtypeStruct((m, e), BF16)],
        compiler_params=_cparams(1),
        name="mla_proj",
    )(x, *consts, cos_t, sin_t)


def _softmax_step(s, m_ref, l_ref, acc_ref, v):
    m_prev = m_ref[...]
    m_new = jnp.maximum(m_prev, jnp.max(s, axis=-1, keepdims=True))
    alpha = jnp.exp(m_prev - m_new)
    p = jnp.exp(s - m_new)
    l_ref[...] = alpha * l_ref[...] + jnp.sum(p, axis=-1, keepdims=True)
    acc_ref[...] = alpha * acc_ref[...] + _dot(p.astype(BF16), v)
    m_ref[...] = m_new


def _mla_prompt_kernel(q_ref, k_ref, o_ref, m_ref, l_ref, acc_ref, *, tk, kv_lora, scale):
    i = pl.program_id(0)
    heads = q_ref.shape[1]
    rows = heads * Q_BLOCK
    q = q_ref[0].reshape(rows, q_ref.shape[3])
    m_ref[...] = jnp.full(m_ref.shape, NEG_INF, F32)
    l_ref[...] = jnp.zeros(l_ref.shape, F32)
    acc_ref[...] = jnp.zeros(acc_ref.shape, F32)
    n_full = (i * Q_BLOCK) // tk

    def tile(start, masked):
        k = k_ref[pl.ds(start, tk), :]
        s = _dot_nt(q, k) * scale
        if masked:
            q_pos = i * Q_BLOCK + (lax.broadcasted_iota(jnp.int32, s.shape, 0) & (Q_BLOCK - 1))
            k_pos = start + lax.broadcasted_iota(jnp.int32, s.shape, 1)
            s = jnp.where(k_pos <= q_pos, s, NEG_INF)
        _softmax_step(s, m_ref, l_ref, acc_ref, k[:, :kv_lora])

    def body(j, carry):
        tile(pl.multiple_of(j * tk, tk), False)
        return carry

    lax.fori_loop(0, n_full, body, 0)
    tile(pl.multiple_of(n_full * tk, tk), True)
    o_ref[0] = (acc_ref[...] / l_ref[...]).reshape(heads, Q_BLOCK, kv_lora)


def _mla_prompt_attn(q, rows_bf, kv_lora, scale, tk=512):
    nb, heads, _, e = q.shape
    s = rows_bf.shape[0]
    tk = min(tk, s)
    rows = heads * Q_BLOCK
    return pl.pallas_call(
        functools.partial(_mla_prompt_kernel, tk=tk, kv_lora=kv_lora, scale=scale),
        grid=(nb,),
        in_specs=[pl.BlockSpec((1, heads, Q_BLOCK, e), lambda i: (i, 0, 0, 0)), _const_spec((s, e))],
        out_specs=pl.BlockSpec((1, heads, Q_BLOCK, kv_lora), lambda i: (i, 0, 0, 0)),
        out_shape=jax.ShapeDtypeStruct((nb, heads, Q_BLOCK, kv_lora), F32),
        scratch_shapes=[pltpu.VMEM((rows, 1), F32), pltpu.VMEM((rows, 1), F32), pltpu.VMEM((rows, kv_lora), F32)],
        compiler_params=_cparams(1, VMEM_LIMIT),
        name="mla_prompt_attn",
    )(q, rows_bf)


def _mla_sample_kernel(pt_ref, q_ref, new_ref, *rest, n_group, dec_seq, kv_lora, scale):
    del pt_ref
    pages = rest[:n_group]
    o_ref, kbuf_ref, nbuf_ref, m_ref, l_ref, acc_ref = rest[n_group:]
    p_idx = pl.program_id(1)
    page = kbuf_ref.shape[0] // n_group
    q = q_ref[...]

    @pl.when(p_idx == 0)
    def _():
        nbuf_ref[...] = jnp.zeros(nbuf_ref.shape, F32)
        nbuf_ref[:dec_seq, :] = new_ref[...]
        nk = nbuf_ref[...].astype(BF16)
        s = _dot_nt(q, nk) * scale
        tok = _irem(lax.broadcasted_iota(jnp.int32, s.shape, 0), dec_seq)
        s = jnp.where(lax.broadcasted_iota(jnp.int32, s.shape, 1) <= tok, s, NEG_INF)
        m0 = jnp.max(s, axis=-1, keepdims=True)
        p0 = jnp.exp(s - m0)
        m_ref[...] = m0
        l_ref[...] = jnp.sum(p0, axis=-1, keepdims=True)
        acc_ref[...] = _dot(p0.astype(BF16), nk[:, :kv_lora])

    for g in range(n_group):
        kbuf_ref[g * page:(g + 1) * page, :] = pages[g][...].astype(BF16)
    k = kbuf_ref[...]
    _softmax_step(_dot_nt(q, k) * scale, m_ref, l_ref, acc_ref, k[:, :kv_lora])

    @pl.when(p_idx == pl.num_programs(1) - 1)
    def _():
        o_ref[...] = acc_ref[...] / l_ref[...]


def _mla_sample_attn(q, new_rows, cache, layer, page_table, kv_lora, scale, n_group=8):
    b, qrows, e = q.shape
    dec_seq = new_rows.shape[1]
    n_pages = page_table.shape[1]
    page = cache.shape[2]
    n_group = min(n_group, n_pages)
    assert n_pages % n_group == 0 and dec_seq <= page

    def page_spec(g):
        return pl.BlockSpec((None, None, page, e),
                            lambda bi, pi, pt: (layer, pt[bi * n_pages + pi * n_group + g], 0, 0))

    grid_spec = pltpu.PrefetchScalarGridSpec(
        num_scalar_prefetch=1,
        grid=(b, n_pages // n_group),
        in_specs=[pl.BlockSpec((None, qrows, e), lambda bi, pi, pt: (bi, 0, 0)),
                  pl.BlockSpec((None, dec_seq, e), lambda bi, pi, pt: (bi, 0, 0))]
                 + [page_spec(g) for g in range(n_group)],
        out_specs=pl.BlockSpec((None, qrows, kv_lora), lambda bi, pi, pt: (bi, 0, 0)),
        scratch_shapes=[pltpu.VMEM((n_group * page, e), BF16), pltpu.VMEM((page, e), F32),
                        pltpu.VMEM((qrows, 1), F32), pltpu.VMEM((qrows, 1), F32), pltpu.VMEM((qrows, kv_lora), F32)],
    )
    return pl.pallas_call(
        functools.partial(_mla_sample_kernel, n_group=n_group, dec_seq=dec_seq, kv_lora=kv_lora, scale=scale),
        grid_spec=grid_spec,
        out_shape=jax.ShapeDtypeStruct((b, qrows, kv_lora), F32),
        compiler_params=_cparams(2),
        name="mla_sample_attn",
    )(page_table.reshape(-1), q, new_rows, *([cache] * n_group))


def _mla_out_kernel(o_ref, x_ref, wuv_ref, wo_ref, y_ref):
    acc = x_ref[...]
    for hd in range(o_ref.shape[1]):
        o_h = _dot(o_ref[0, hd].astype(BF16), wuv_ref[hd]).astype(BF16)
        acc = acc + _dot(o_h, wo_ref[hd])
    y_ref[...] = acc


def _mla_out(o_lat, x, w_uv, w_o):
    nb, heads, tq, kv_lora = o_lat.shape
    d = x.shape[1]
    return pl.pallas_call(
        _mla_out_kernel,
        grid=(nb,),
        in_specs=[pl.BlockSpec((1, heads, tq, kv_lora), lambda i: (i, 0, 0, 0)),
                  pl.BlockSpec((tq, d), lambda i: (i, 0)),
                  _const_spec(w_uv.shape), _const_spec(w_o.shape)],
        out_specs=pl.BlockSpec((tq, d), lambda i: (i, 0)),
        out_shape=jax.ShapeDtypeStruct(x.shape, F32),
        compiler_params=_cparams(1),
        name="mla_out",
    )(o_lat, x, w_uv, w_o)


def _norm_linear_kernel(x_ref, g_ref, w_ref, b_ref, cos_ref, sin_ref, y_ref, *, rope_cols):
    h = _rms(x_ref[...], g_ref[...]).astype(BF16)
    y = _dot(h, w_ref[...]) + b_ref[...]
    cos_t, sin_t = cos_ref[...], sin_ref[...]
    for c in range(rope_cols // LANES):
        y_ref[:, c * LANES:(c + 1) * LANES] = _rope128(y[:, c * LANES:(c + 1) * LANES], cos_t, sin_t)
    if rope_cols < y.shape[1]:
        y_ref[:, rope_cols:] = y[:, rope_cols:]


def _norm_linear(x, g, w, b, cos_t, sin_t, rope_cols, tm=512):
    m, d = x.shape
    n = w.shape[1]
    tm = min(tm, m)
    row_spec = lambda width: pl.BlockSpec((tm, width), lambda i: (i, 0))
    return pl.pallas_call(
        functools.partial(_norm_linear_kernel, rope_cols=rope_cols),
        grid=(m // tm,),
        in_specs=[row_spec(d), _const_spec((1, d)), _const_spec((d, n)), _const_spec((1, n)),
                  row_spec(LANES), row_spec(LANES)],
        out_specs=row_spec(n),
        out_shape=jax.ShapeDtypeStruct((m, n), F32),
        compiler_params=_cparams(1),
        name="norm_linear_rope",
    )(x, g.reshape(1, d), w, b.reshape(1, n), cos_t, sin_t)


def _sink_attend(s, valid, sink, v):
    s = jnp.where(valid, s, NEG_INF)
    m = jnp.maximum(jnp.max(s, axis=-1, keepdims=True), sink)
    p = jnp.exp(s - m)
    pr = p / (jnp.sum(p, axis=-1, keepdims=True) + jnp.exp(sink - m))
    return _dot(pr.astype(BF16), v)


def _stack_group(q, kvh, group, hd):
    return jnp.concatenate([q[:, (kvh * group + g) * hd:(kvh * group + g + 1) * hd] for g in range(group)], axis=0)


def _sink_rows(sinks_ref, kvh, group, t):
    row = lax.broadcasted_iota(jnp.int32, (group * t, 1), 0)
    sink = jnp.full((group * t, 1), sinks_ref[kvh * group], F32)
    for g in range(1, group):
        sink = jnp.where(row >= g * t, sinks_ref[kvh * group + g], sink)
    return sink


def _swa_finish(o_scr, x_ref, wo_ref, bo_ref, y_ref):
    y_ref[...] = x_ref[...] + _dot(o_scr[...].astype(BF16), wo_ref[...]) + bo_ref[...]


def _swa_prompt_kernel(sinks_ref, q_ref, cur_ref, pk_ref, pv_ref, x_ref, wo_ref, bo_ref, y_ref, o_scr,
                       *, kv_heads, group, hd, scale):
    i = pl.program_id(0)
    t = q_ref.shape[0]
    kw = kv_heads * hd
    q = q_ref[...]
    kk = jnp.concatenate([pk_ref[...], cur_ref[:, :kw]], axis=0).astype(BF16)
    vv = jnp.concatenate([pv_ref[...], cur_ref[:, kw:]], axis=0).astype(BF16)
    shape = (group * t, 2 * t)
    r = _irem(lax.broadcasted_iota(jnp.int32, shape, 0), t)
    j = lax.broadcasted_iota(jnp.int32, shape, 1)
    valid = (j > r) & (j <= r + t) & ((j >= t) | (i > 0))
    for kvh in range(kv_heads):
        qg = _stack_group(q, kvh, group, hd).astype(BF16)
        s = _dot_nt(qg, kk[:, kvh * hd:(kvh + 1) * hd]) * scale
        o = _sink_attend(s, valid, _sink_rows(sinks_ref, kvh, group, t), vv[:, kvh * hd:(kvh + 1) * hd])
        for g in range(group):
            hh = kvh * group + g
            o_scr[:, hh * hd:(hh + 1) * hd] = o[g * t:(g + 1) * t]
    _swa_finish(o_scr, x_ref, wo_ref, bo_ref, y_ref)


def _swa_prompt_attn(q, kv, x, sinks, w_o, b_o, kv_heads, hd, scale):
    m, d = x.shape
    qw = q.shape[1]
    kw = kv_heads * hd
    t = Q_BLOCK
    group = qw // kw
    prev = lambda col: pl.BlockSpec((t, kw), lambda i: (jnp.maximum(i - 1, 0), col))
    return pl.pallas_call(
        functools.partial(_swa_prompt_kernel, kv_heads=kv_heads, group=group, hd=hd, scale=scale),
        grid=(m // t,),
        in_specs=[pl.BlockSpec(memory_space=pltpu.SMEM),
                  pl.BlockSpec((t, qw), lambda i: (i, 0)),
                  pl.BlockSpec((t, 2 * kw), lambda i: (i, 0)),
                  prev(0), prev(1),
                  pl.BlockSpec((t, d), lambda i: (i, 0)),
                  _const_spec(w_o.shape), _const_spec((1, d))],
        out_specs=pl.BlockSpec((t, d), lambda i: (i, 0)),
        out_shape=jax.ShapeDtypeStruct((m, d), F32),
        scratch_shapes=[pltpu.VMEM((t, qw), F32)],
        compiler_params=_cparams(1),
        name="swa_prompt_attn",
    )(sinks, q, kv, kv, kv, x, w_o, b_o.reshape(1, d))


def _swa_sample_kernel(sinks_ref, q_ref, new_ref, ck_ref, cv_ref, x_ref, wo_ref, bo_ref, y_ref, o_scr,
                       *, kv_heads, group, hd, dec_seq, scale):
    n_seq, wb, kw = ck_ref.shape
    t = n_seq * dec_seq
    q = q_ref[...]
    kk = jnp.concatenate([ck_ref[...].reshape(n_seq * wb, kw), new_ref[:, :kw]], axis=0).astype(BF16)
    vv = jnp.concatenate([cv_ref[...].reshape(n_seq * wb, kw), new_ref[:, kw:]], axis=0).astype(BF16)
    n_old = n_seq * wb
    shape = (group * t, n_old + t)
    r = _irem(lax.broadcasted_iota(jnp.int32, shape, 0), t)
    r_seq, r_tok = _idiv(r, dec_seq), _irem(r, dec_seq)
    j = lax.broadcasted_iota(jnp.int32, shape, 1)
    old = j < n_old
    j_new = jnp.maximum(j - n_old, 0)
    j_seq = jnp.where(old, _idiv(j, wb), _idiv(j_new, dec_seq))
    j_pos = jnp.where(old, _irem(j, wb), wb + _irem(j_new, dec_seq))
    valid = (j_seq == r_seq) & (j_pos > r_tok) & (j_pos <= r_tok + wb)
    for kvh in range(kv_heads):
        qg = _stack_group(q, kvh, group, hd).astype(BF16)
        s = _dot_nt(qg, kk[:, kvh * hd:(kvh + 1) * hd]) * scale
        o = _sink_attend(s, valid, _sink_rows(sinks_ref, kvh, group, t), vv[:, kvh * hd:(kvh + 1) * hd])
        for g in range(group):
            hh = kvh * group + g
            o_scr[:, hh * hd:(hh + 1) * hd] = o[g * t:(g + 1) * t]
    _swa_finish(o_scr, x_ref, wo_ref, bo_ref, y_ref)


def _swa_sample_attn(q, kv_new, cache_k, cache_v, x, sinks, w_o, b_o, kv_heads, hd, dec_seq, scale, n_seq=8):
    m, d = x.shape
    qw = q.shape[1]
    kw = kv_heads * hd
    b, wb = cache_k.shape[:2]
    n_seq = min(n_seq, b)
    t = n_seq * dec_seq
    group = qw // kw
    return pl.pallas_call(
        functools.partial(_swa_sample_kernel, kv_heads=kv_heads, group=group, hd=hd, dec_seq=dec_seq, scale=scale),
        grid=(b // n_seq,),
        in_specs=[pl.BlockSpec(memory_space=pltpu.SMEM),
                  pl.BlockSpec((t, qw), lambda i: (i, 0)),
                  pl.BlockSpec((t, 2 * kw), lambda i: (i, 0)),
                  pl.BlockSpec((n_seq, wb, kw), lambda i: (i, 0, 0)),
                  pl.BlockSpec((n_seq, wb, kw), lambda i: (i, 0, 0)),
                  pl.BlockSpec((t, d), lambda i: (i, 0)),
                  _const_spec(w_o.shape), _const_spec((1, d))],
        out_specs=pl.BlockSpec((t, d), lambda i: (i, 0)),
        out_shape=jax.ShapeDtypeStruct((m, d), F32),
        scratch_shapes=[pltpu.VMEM((t, qw), F32)],
        compiler_params=_cparams(1),
        name="swa_sample_attn",
    )(sinks, q, kv_new, cache_k.reshape(b, wb, kw), cache_v.reshape(b, wb, kw), x, w_o, b_o.reshape(1, d))


def _prep_weights(p):
    n_a, kv_lora, heads, nope = p["mla_w_uk"].shape
    q_lora = p["mla_w_dq"].shape[2]
    d = p["mla_w_o"].shape[2]
    v_dim = p["mla_w_uv"].shape[3]
    w = {"ffn_in": p["ffn_w_in"].astype(BF16), "ffn_out": p["ffn_w_out"].astype(BF16), "mla": []}
    for l in range(n_a):
        w_uq = p["mla_w_uq"][l].reshape(q_lora, heads, nope + ROPE_DIM)
        w_dkv = p["mla_w_dkv"][l]
        w["mla"].append({
            "norm": p["mla_norm"][l], "q_norm": p["mla_q_norm"][l], "kv_norm": p["mla_kv_norm"][l],
            "w_dq": p["mla_w_dq"][l].astype(BF16),
            "w_uq_n": w_uq[:, :, :nope].reshape(q_lora, heads * nope).astype(BF16),
            "w_uq_r": w_uq[:, :, nope:].reshape(q_lora, heads * ROPE_DIM).astype(BF16),
            "w_uk": jnp.transpose(p["mla_w_uk"][l], (1, 2, 0)).astype(BF16),
            "w_dkv_c": w_dkv[:, :kv_lora].astype(BF16),
            "w_dkv_r": jnp.pad(w_dkv[:, kv_lora:], ((0, 0), (0, LANES - ROPE_DIM))).astype(BF16),
            "w_uv": jnp.transpose(p["mla_w_uv"][l], (1, 0, 2)).astype(BF16),
            "w_o": p["mla_w_o"][l].reshape(heads, v_dim, d).astype(BF16),
        })
    w["swa_kv"] = p["swa_w_kv"].astype(BF16)
    w["swa_q"] = p["swa_w_q"].astype(BF16)
    w["swa_o"] = p["swa_w_o"].astype(BF16)
    return w


def _run_trunk(x, pos, p, w, mla_attend, swa_attend):
    depth = p["ffn_norm"].shape[0]
    n_a = p["mla_norm"].shape[0]
    kv_w = w["swa_kv"].shape[1]
    cos_t, sin_t = _rope_tables(pos)
    rows_all, kv = [], None
    for l in range(depth):
        if l == n_a:
            kv = _norm_linear(x, p["kv_norm"], w["swa_kv"], p["swa_b_kv"], cos_t, sin_t, rope_cols=kv_w // 2)
        x = _ffn(x, p["ffn_norm"][l, 0], w["ffn_in"][l, 0], w["ffn_out"][l, 0])
        if l < n_a:
            wl = w["mla"][l]
            q, rows, rows_bf = _mla_proj(x, wl, cos_t, sin_t)
            o_lat = mla_attend(l, q, rows, rows_bf)
            x = _mla_out(o_lat, x, wl["w_uv"], wl["w_o"])
            rows_all.append(rows)
        else:
            jl = l - n_a
            q = _norm_linear(x, p["swa_norm"][jl], w["swa_q"][jl], p["swa_b_q"][jl], cos_t, sin_t,
                             rope_cols=w["swa_q"].shape[2])
            x = swa_attend(q, kv, x, p["swa_sinks"][jl], w["swa_o"][jl], p["swa_b_o"][jl])
        x = _ffn(x, p["ffn_norm"][l, 1], w["ffn_in"][l, 1], w["ffn_out"][l, 1],
                 final_g=p["final_norm"] if l == depth - 1 else None)
    return x, jnp.stack(rows_all, axis=0), kv


def kernel(x_prompt, x_sample, cache_mla, cache_swa_k, cache_swa_v, page_table, ffn_norm, ffn_w_in, ffn_w_out, mla_norm, mla_w_dq, mla_q_norm, mla_w_uq, mla_w_dkv, mla_kv_norm, mla_w_uk, mla_w_uv, mla_w_o, kv_norm, swa_w_kv, swa_b_kv, swa_norm, swa_w_q, swa_b_q, swa_sinks, swa_w_o, swa_b_o, final_norm):
    p = {
        "ffn_norm": ffn_norm, "ffn_w_in": ffn_w_in, "ffn_w_out": ffn_w_out,
        "mla_norm": mla_norm, "mla_w_dq": mla_w_dq, "mla_q_norm": mla_q_norm, "mla_w_uq": mla_w_uq,
        "mla_w_dkv": mla_w_dkv, "mla_kv_norm": mla_kv_norm, "mla_w_uk": mla_w_uk, "mla_w_uv": mla_w_uv,
        "mla_w_o": mla_w_o,
        "kv_norm": kv_norm, "swa_w_kv": swa_w_kv, "swa_b_kv": swa_b_kv,
        "swa_norm": swa_norm, "swa_w_q": swa_w_q, "swa_b_q": swa_b_q, "swa_sinks": swa_sinks,
        "swa_w_o": swa_w_o, "swa_b_o": swa_b_o, "final_norm": final_norm,
    }
    batch, seq, d = x_prompt.shape
    dec_batch, dec_seq, _ = x_sample.shape
    _, kv_lora, heads, nope = mla_w_uk.shape
    e = kv_lora + ROPE_DIM
    kv_heads, hd = cache_swa_k.shape[2:]
    past_len = page_table.shape[1] * cache_mla.shape[2]
    mla_scale = float(nope + ROPE_DIM) ** -0.5
    swa_scale = float(hd) ** -0.5
    assert batch == 1 and seq % Q_BLOCK == 0 and cache_swa_k.shape[1] == Q_BLOCK
    w = _prep_weights(p)

    def mla_prompt(l, q, rows, rows_bf):
        return _mla_prompt_attn(q, rows_bf, kv_lora, mla_scale)

    def swa_prompt(q, kv, x, sinks, w_o, b_o):
        return _swa_prompt_attn(q, kv, x, sinks, w_o, b_o, kv_heads, hd, swa_scale)

    y_p, rows_p, kv_p = _run_trunk(x_prompt.reshape(seq, d), jnp.arange(seq, dtype=jnp.int32), p, w,
                                   mla_prompt, swa_prompt)

    n_tok = dec_batch * dec_seq

    def mla_sample(l, q, rows, rows_bf):
        q_tok = jnp.transpose(q, (0, 2, 1, 3)).reshape(dec_batch, dec_seq, heads, e)
        q_seq = jnp.transpose(q_tok, (0, 2, 1, 3)).reshape(dec_batch, heads * dec_seq, e)
        o = _mla_sample_attn(q_seq, rows.reshape(dec_batch, dec_seq, e), cache_mla, l, page_table, kv_lora, mla_scale)
        o = jnp.transpose(o.reshape(dec_batch, heads, dec_seq, kv_lora), (1, 0, 2, 3))
        return o.reshape(1, heads, n_tok, kv_lora)

    def swa_sample(q, kv, x, sinks, w_o, b_o):
        return _swa_sample_attn(q, kv, cache_swa_k, cache_swa_v, x, sinks, w_o, b_o, kv_heads, hd, dec_seq, swa_scale)

    pos_s = jnp.tile(past_len + jnp.arange(dec_seq, dtype=jnp.int32), dec_batch)
    y_s, rows_s, kv_s = _run_trunk(x_sample.reshape(n_tok, d), pos_s, p, w, mla_sample, swa_sample)

    kw = kv_heads * hd
    w_p = min(Q_BLOCK, seq)
    k_p = kv_p[seq - w_p:, :kw].reshape(1, w_p, kv_heads, hd)
    v_p = kv_p[seq - w_p:, kw:].reshape(1, w_p, kv_heads, hd)
    k_s = kv_s[:, :kw].reshape(dec_batch, dec_seq, kv_heads, hd)
    v_s = kv_s[:, kw:].reshape(dec_batch, dec_seq, kv_heads, hd)
    swa_k_sample = jnp.concatenate([cache_swa_k, k_s], axis=1)[:, dec_seq:]
    swa_v_sample = jnp.concatenate([cache_swa_v, v_s], axis=1)[:, dec_seq:]
    return (y_p.reshape(1, seq, d), y_s.reshape(dec_batch, dec_seq, d),
            rows_p.reshape(-1, 1, seq, e), rows_s.reshape(-1, dec_batch, dec_seq, e),
            k_p, v_p, swa_k_sample, swa_v_sample)
```

```python
import functools

import jax
import jax.numpy as jnp
from jax import lax
from jax.experimental import pallas as pl
from jax.experimental.pallas import tpu as pltpu

F32 = jnp.float32
BF16 = jnp.bfloat16

EPS = 1e-6
ROPE_THETA = 10000.0
FFN_RES_WEIGHT = 0.5
LANES = 128
ROPE_DIM = 64
Q_BLOCK = 128
VMEM_LIMIT = 56 * 1024 * 1024
NEG_INF = float("-inf")
LOG2_E = 1.4426950408889634


def _cparams(n_axes, vmem=None):
    return pltpu.CompilerParams(dimension_semantics=("arbitrary",) * n_axes, vmem_limit_bytes=vmem)


def _const_spec(shape):
    nd = len(shape)
    return pl.BlockSpec(shape, lambda *_: (0,) * nd, pipeline_mode=pl.Buffered(1))


def _dot(a, b):
    return jnp.dot(a, b, preferred_element_type=F32)


def _dot_nt(a, b):
    return lax.dot_general(a, b, (((1,), (1,)), ((), ())), preferred_element_type=F32)


def _idiv(x, n):
    if n & (n - 1) == 0:
        return x >> (n.bit_length() - 1)
    return lax.div(x, jnp.full(x.shape, n, x.dtype))


def _irem(x, n):
    if n & (n - 1) == 0:
        return x & (n - 1)
    return lax.rem(x, jnp.full(x.shape, n, x.dtype))


def _rms(x, g):
    return x * lax.rsqrt(jnp.mean(x * x, axis=-1, keepdims=True) + EPS) * g


def _rope128(x, cos_t, sin_t):
    lane = lax.broadcasted_iota(jnp.int32, x.shape, 1)
    first_half = (lane & (ROPE_DIM // 2)) == 0
    partner = jnp.where(first_half, pltpu.roll(x, LANES - ROPE_DIM // 2, 1), pltpu.roll(x, ROPE_DIM // 2, 1))
    return x * cos_t + partner * sin_t


def _rope_tables(pos):
    half = ROPE_DIM // 2
    inv_freq = jnp.exp(-(jnp.arange(half, dtype=F32) / half) * jnp.log(F32(ROPE_THETA)))
    ang = pos.astype(F32)[:, None] * inv_freq[None, :]
    cos, sin = jnp.cos(ang), jnp.sin(ang)
    reps = LANES // ROPE_DIM
    cos_t = jnp.tile(cos, (1, 2 * reps))
    sin_t = jnp.tile(jnp.concatenate([-sin, sin], axis=1), (1, reps))
    return cos_t, sin_t


def _ffn_kernel(x_ref, g_ref, win_ref, wout_ref, *rest, d_ff, n_chunks, final):
    if final:
        gf_ref, o_ref = rest
    else:
        (o_ref,) = rest
    x = x_ref[...]
    h = _rms(x, g_ref[...]).astype(BF16)
    ck = d_ff // n_chunks
    acc = None
    for c in range(n_chunks):
        gate = _dot(h, win_ref[:, c * ck:(c + 1) * ck])
        up = _dot(h, win_ref[:, d_ff + c * ck:d_ff + (c + 1) * ck])
        act = (gate * (1.0 / (1.0 + jnp.exp(-gate))) * up).astype(BF16)
        part = _dot(act, wout_ref[c * ck:(c + 1) * ck, :])
        acc = part if acc is None else acc + part
    y = x + FFN_RES_WEIGHT * acc
    if final:
        y = _rms(y, gf_ref[...])
    o_ref[...] = y


def _ffn(x, g, w_in, w_out, final_g=None, tm=512):
    m, d = x.shape
    d_ff = w_out.shape[0]
    tm = min(tm, m)
    n_chunks = 2 if d_ff % (2 * LANES) == 0 else 1
    in_specs = [
        pl.BlockSpec((tm, d), lambda i: (i, 0)),
        _const_spec((1, d)),
        _const_spec((d, 2 * d_ff)),
        _const_spec((d_ff, d)),
    ]
    args = [x, g.reshape(1, d), w_in, w_out]
    if final_g is not None:
        in_specs.append(_const_spec((1, d)))
        args.append(final_g.reshape(1, d))
    return pl.pallas_call(
        functools.partial(_ffn_kernel, d_ff=d_ff, n_chunks=n_chunks, final=final_g is not None),
        grid=(m // tm,),
        in_specs=in_specs,
        out_specs=pl.BlockSpec((tm, d), lambda i: (i, 0)),
        out_shape=jax.ShapeDtypeStruct((m, d), F32),
        compiler_params=_cparams(1, VMEM_LIMIT),
        name="ffn_half",
    )(*args)


def _mla_proj_kernel(x_ref, g_ref, wdq_ref, gq_ref, wuqn_ref, wuqr_ref, wuk_ref, wdkvc_ref, wdkvr_ref,
                     gkv_ref, cos_ref, sin_ref, q_ref, rows_ref, rowsb_ref, *, heads, nope, kv_lora, q_scale):
    tm = x_ref.shape[0]
    cos_t, sin_t = cos_ref[...], sin_ref[...]
    h = _rms(x_ref[...], g_ref[...]).astype(BF16)
    cq = _rms(_dot(h, wdq_ref[...]), gq_ref[...]).astype(BF16)
    qn = _dot(cq, wuqn_ref[...])
    qr = _dot(cq, wuqr_ref[...])
    qr = jnp.concatenate(
        [_rope128(qr[:, c * LANES:(c + 1) * LANES], cos_t, sin_t) for c in range(qr.shape[1] // LANES)], axis=1)
    c_lat = _rms(_dot(h, wdkvc_ref[...]), gkv_ref[...])
    k_pe = _rope128(_dot(h, wdkvr_ref[...]), cos_t, sin_t)[:, :ROPE_DIM]
    rows_ref[:, :kv_lora] = c_lat
    rows_ref[:, kv_lora:] = k_pe
    rowsb_ref[:, :kv_lora] = c_lat.astype(BF16)
    rowsb_ref[:, kv_lora:] = k_pe.astype(BF16)
    for hd in range(heads):
        q_lat = (_dot(qn[:, hd * nope:(hd + 1) * nope].astype(BF16), wuk_ref[hd]) * q_scale).astype(BF16)
        q_pe = (qr[:, hd * ROPE_DIM:(hd + 1) * ROPE_DIM] * q_scale).astype(BF16)
        for r in range(tm // Q_BLOCK):
            q_ref[r, hd, :, :kv_lora] = q_lat[r * Q_BLOCK:(r + 1) * Q_BLOCK]
            q_ref[r, hd, :, kv_lora:] = q_pe[r * Q_BLOCK:(r + 1) * Q_BLOCK]


def _mla_proj(x, w, cos_t, sin_t, q_scale, tm=256):
    m, d = x.shape
    heads, nope, kv_lora = w["w_uk"].shape
    e = kv_lora + ROPE_DIM
    q_lora = w["w_dq"].shape[1]
    tm = min(tm, m)
    nb = tm // Q_BLOCK
    consts = [w["norm"].reshape(1, d), w["w_dq"], w["q_norm"].reshape(1, q_lora), w["w_uq_n"], w["w_uq_r"],
              w["w_uk"], w["w_dkv_c"], w["w_dkv_r"], w["kv_norm"].reshape(1, kv_lora)]
    row_spec = lambda width: pl.BlockSpec((tm, width), lambda i: (i, 0))
    return pl.pallas_call(
        functools.partial(_mla_proj_kernel, heads=heads, nope=nope, kv_lora=kv_lora, q_scale=q_scale),
        grid=(m // tm,),
        in_specs=[row_spec(d)] + [_const_spec(c.shape) for c in consts] + [row_spec(LANES), row_spec(LANES)],
        out_specs=[pl.BlockSpec((nb, heads, Q_BLOCK, e), lambda i: (i, 0, 0, 0)), row_spec(e), row_spec(e)],
        out_shape=[jax.ShapeDtypeStruct((m // Q_BLOCK, heads, Q_BLOCK, e), BF16),
                   jax.ShapeDtypeStruct((m, e), F32),
                   jax.ShapeDtypeStruct((m, e), BF16)],
        compiler_params=_cparams(1),
        name="mla_proj",
    )(x, *consts, cos_t, sin_t)


def _flash_tile(q_ref, k_ref, start, q_pos0, m_ref, l_ref, acc_ref, *, tk, kv_lora, n_split):
    heads, _, e = q_ref.shape[1:]
    hs = heads // n_split
    rc = hs * Q_BLOCK
    k = k_ref[pl.ds(start, tk), :]
    v = k[:, :kv_lora]
    for c in range(n_split):
        r0, r1 = c * rc, (c + 1) * rc
        s = _dot_nt(q_ref[0, c * hs:(c + 1) * hs].reshape(rc, e), k)
        if q_pos0 is not None:
            q_pos = q_pos0 + (lax.broadcasted_iota(jnp.int32, s.shape, 0) & (Q_BLOCK - 1))
            k_pos = start + lax.broadcasted_iota(jnp.int32, s.shape, 1)
            s = jnp.where(k_pos <= q_pos, s, NEG_INF)
        chunks = [s[:, j * LANES:(j + 1) * LANES] for j in range(tk // LANES)]
        mx = functools.reduce(jnp.maximum, chunks)
        m_prev = m_ref[r0:r1, :]
        m_new = jnp.maximum(m_prev, jnp.max(mx, axis=-1, keepdims=True))
        alpha = jnp.exp2(m_prev - m_new)
        ps = [jnp.exp2(ch - m_new) for ch in chunks]
        l_ref[r0:r1, :] = alpha * l_ref[r0:r1, :] + functools.reduce(jnp.add, ps)
        pv = _dot(jnp.concatenate(ps, axis=1).astype(BF16), v)
        acc_ref[r0:r1, :] = acc_ref[r0:r1, :] * jnp.concatenate([alpha] * (kv_lora // LANES), axis=1) + pv
        m_ref[r0:r1, :] = m_new


def _mla_prompt_kernel(q_ref, k_ref, o_ref, m_ref, l_ref, acc_ref, *, tk, kv_lora, n_split):
    i = pl.program_id(0)
    heads = q_ref.shape[1]
    m_ref[...] = jnp.full(m_ref.shape, NEG_INF, F32)
    l_ref[...] = jnp.zeros(l_ref.shape, F32)
    acc_ref[...] = jnp.zeros(acc_ref.shape, F32)
    tile = functools.partial(_flash_tile, q_ref, k_ref, m_ref=m_ref, l_ref=l_ref, acc_ref=acc_ref,
                             tk=tk, kv_lora=kv_lora, n_split=n_split)
    n_full = (i * Q_BLOCK) // tk

    def body(j, carry):
        tile(pl.multiple_of(j * tk, tk), None)
        return carry

    lax.fori_loop(0, n_full, body, 0)
    tile(pl.multiple_of(n_full * tk, tk), i * Q_BLOCK)
    l_tot = jnp.sum(l_ref[...], axis=-1, keepdims=True)
    o_ref[0] = (acc_ref[...] / l_tot).reshape(heads, Q_BLOCK, kv_lora)


def _mla_prompt_attn(q, rows_bf, kv_lora, tk=1024, n_split=2):
    nb, heads, _, e = q.shape
    s = rows_bf.shape[0]
    tk = min(tk, s)
    assert s % tk == 0 and tk % Q_BLOCK == 0 and heads % n_split == 0
    rows = heads * Q_BLOCK
    return pl.pallas_call(
        functools.partial(_mla_prompt_kernel, tk=tk, kv_lora=kv_lora, n_split=n_split),
        grid=(nb,),
        in_specs=[pl.BlockSpec((1, heads, Q_BLOCK, e), lambda i: (i, 0, 0, 0)), _const_spec((s, e))],
        out_specs=pl.BlockSpec((1, heads, Q_BLOCK, kv_lora), lambda i: (i, 0, 0, 0)),
        out_shape=jax.ShapeDtypeStruct((nb, heads, Q_BLOCK, kv_lora), F32),
        scratch_shapes=[pltpu.VMEM((rows, LANES), F32), pltpu.VMEM((rows, LANES), F32),
                        pltpu.VMEM((rows, kv_lora), F32)],
        compiler_params=_cparams(1, VMEM_LIMIT),
        name="mla_prompt_attn",
    )(q, rows_bf)


def _mla_sample_kernel(pt_ref, q_ref, new_ref, *rest, n_group, dec_seq, kv_lora):
    del pt_ref
    pages = rest[:n_group]
    o_ref, kbuf_ref, nbuf_ref, m_ref, l_ref, acc_ref = rest[n_group:]
    p_idx = pl.program_id(1)
    page = kbuf_ref.shape[1] // n_group
    q = q_ref[...]

    @pl.when(p_idx == 0)
    def _():
        nbuf_ref[...] = jnp.zeros(nbuf_ref.shape, F32)
        nbuf_ref[:dec_seq, :] = new_ref[...]
        nk = nbuf_ref[...].astype(BF16)
        s = _dot_nt(q, nk)
        tok = _irem(lax.broadcasted_iota(jnp.int32, s.shape, 0), dec_seq)
        s = jnp.where(lax.broadcasted_iota(jnp.int32, s.shape, 1) <= tok, s, NEG_INF)
        m0 = jnp.max(s, axis=-1, keepdims=True)
        p0 = jnp.exp2(s - m0)
        m_ref[...] = m0
        l_ref[...] = jnp.sum(p0, axis=-1, keepdims=True)
        acc_ref[...] = _dot(p0.astype(BF16), nk[:, :kv_lora])

    for g in range(n_group):
        kbuf_ref[:, g * page:(g + 1) * page] = pages[g][...].astype(BF16)
    kt = kbuf_ref[...]
    s = _dot(q, kt)
    m_prev = m_ref[...]
    m_new = jnp.maximum(m_prev, jnp.max(s, axis=-1, keepdims=True))
    alpha = jnp.exp2(m_prev - m_new)
    p = jnp.exp2(s - m_new)
    l_ref[...] = alpha * l_ref[...] + jnp.sum(p, axis=-1, keepdims=True)
    acc_ref[...] = alpha * acc_ref[...] + _dot_nt(p.astype(BF16), kt[:kv_lora, :])
    m_ref[...] = m_new

    @pl.when(p_idx == pl.num_programs(1) - 1)
    def _():
        o_ref[...] = acc_ref[...] / l_ref[...]


def _mla_sample_attn(q, new_rows, cache_t, layer, page_table, kv_lora, n_group=32):
    b, qrows, e = q.shape
    dec_seq = new_rows.shape[1]
    n_pages = page_table.shape[1]
    page = cache_t.shape[3]
    n_group = min(n_group, n_pages)
    assert n_pages % n_group == 0 and dec_seq <= page

    def page_spec(g):
        return pl.BlockSpec((None, None, e, page),
                            lambda bi, pi, pt: (layer, pt[bi * n_pages + pi * n_group + g], 0, 0))

    grid_spec = pltpu.PrefetchScalarGridSpec(
        num_scalar_prefetch=1,
        grid=(b, n_pages // n_group),
        in_specs=[pl.BlockSpec((None, qrows, e), lambda bi, pi, pt: (bi, 0, 0)),
                  pl.BlockSpec((None, dec_seq, e), lambda bi, pi, pt: (bi, 0, 0))]
                 + [page_spec(g) for g in range(n_group)],
        out_specs=pl.BlockSpec((None, qrows, kv_lora), lambda bi, pi, pt: (bi, 0, 0)),
        scratch_shapes=[pltpu.VMEM((e, n_group * page), BF16), pltpu.VMEM((page, e), F32),
                        pltpu.VMEM((qrows, 1), F32), pltpu.VMEM((qrows, 1), F32), pltpu.VMEM((qrows, kv_lora), F32)],
    )
    return pl.pallas_call(
        functools.partial(_mla_sample_kernel, n_group=n_group, dec_seq=dec_seq, kv_lora=kv_lora),
        grid_spec=grid_spec,
        out_shape=jax.ShapeDtypeStruct((b, qrows, kv_lora), F32),
        compiler_params=_cparams(2),
        name="mla_sample_attn",
    )(page_table.reshape(-1), q, new_rows, *([cache_t] * n_group))


def _mla_out_kernel(o_ref, x_ref, wuv_ref, wo_ref, y_ref):
    nblk, heads, tq, kv_lora = o_ref.shape
    acc = x_ref[...]
    for hd in range(heads):
        o_h = _dot(o_ref[:, hd].reshape(nblk * tq, kv_lora).astype(BF16), wuv_ref[hd]).astype(BF16)
        acc = acc + _dot(o_h, wo_ref[hd])
    y_ref[...] = acc


def _mla_out(o_lat, x, w_uv, w_o, rows_per_step=512):
    nb, heads, tq, kv_lora = o_lat.shape
    d = x.shape[1]
    nblk = max(1, min(nb, rows_per_step // tq))
    assert nb % nblk == 0
    return pl.pallas_call(
        _mla_out_kernel,
        grid=(nb // nblk,),
        in_specs=[pl.BlockSpec((nblk, heads, tq, kv_lora), lambda i: (i, 0, 0, 0)),
                  pl.BlockSpec((nblk * tq, d), lambda i: (i, 0)),
                  _const_spec(w_uv.shape), _const_spec(w_o.shape)],
        out_specs=pl.BlockSpec((nblk * tq, d), lambda i: (i, 0)),
        out_shape=jax.ShapeDtypeStruct(x.shape, F32),
        compiler_params=_cparams(1),
        name="mla_out",
    )(o_lat, x, w_uv, w_o)


def _norm_linear_kernel(x_ref, g_ref, w_ref, b_ref, cos_ref, sin_ref, y_ref, *, rope_cols):
    h = _rms(x_ref[...], g_ref[...]).astype(BF16)
    y = _dot(h, w_ref[...]) + b_ref[...]
    cos_t, sin_t = cos_ref[...], sin_ref[...]
    for c in range(rope_cols // LANES):
        y_ref[:, c * LANES:(c + 1) * LANES] = _rope128(y[:, c * LANES:(c + 1) * LANES], cos_t, sin_t)
    if rope_cols < y.shape[1]:
        y_ref[:, rope_cols:] = y[:, rope_cols:]


def _norm_linear(x, g, w, b, cos_t, sin_t, rope_cols, tm=512):
    m, d = x.shape
    n = w.shape[1]
    tm = min(tm, m)
    row_spec = lambda width: pl.BlockSpec((tm, width), lambda i: (i, 0))
    return pl.pallas_call(
        functools.partial(_norm_linear_kernel, rope_cols=rope_cols),
        grid=(m // tm,),
        in_specs=[row_spec(d), _const_spec((1, d)), _const_spec((d, n)), _const_spec((1, n)),
                  row_spec(LANES), row_spec(LANES)],
        out_specs=row_spec(n),
        out_shape=jax.ShapeDtypeStruct((m, n), F32),
        compiler_params=_cparams(1),
        name="norm_linear_rope",
    )(x, g.reshape(1, d), w, b.reshape(1, n), cos_t, sin_t)


def _sink_attend(s, valid, sink, v):
    s = jnp.where(valid, s, NEG_INF)
    m = jnp.maximum(jnp.max(s, axis=-1, keepdims=True), sink)
    p = jnp.exp(s - m)
    pr = p / (jnp.sum(p, axis=-1, keepdims=True) + jnp.exp(sink - m))
    return _dot(pr.astype(BF16), v)


def _stack_group(q, kvh, group, hd):
    return jnp.concatenate([q[:, (kvh * group + g) * hd:(kvh * group + g + 1) * hd] for g in range(group)], axis=0)


def _sink_rows(sinks_ref, kvh, group, t):
    row = lax.broadcasted_iota(jnp.int32, (group * t, 1), 0)
    sink = jnp.full((group * t, 1), sinks_ref[kvh * group], F32)
    for g in range(1, group):
        sink = jnp.where(row >= g * t, sinks_ref[kvh * group + g], sink)
    return sink


def _swa_finish(o_scr, x_ref, wo_ref, bo_ref, y_ref):
    y_ref[...] = x_ref[...] + _dot(o_scr[...].astype(BF16), wo_ref[...]) + bo_ref[...]


def _swa_prompt_kernel(sinks_ref, q_ref, cur_ref, pk_ref, pv_ref, x_ref, wo_ref, bo_ref, y_ref, o_scr,
                       *, kv_heads, group, hd, scale):
    i = pl.program_id(0)
    t = q_ref.shape[0]
    kw = kv_heads * hd
    q = q_ref[...]
    kk = jnp.concatenate([pk_ref[...], cur_ref[:, :kw]], axis=0).astype(BF16)
    vv = jnp.concatenate([pv_ref[...], cur_ref[:, kw:]], axis=0).astype(BF16)
    shape = (group * t, 2 * t)
    r = _irem(lax.broadcasted_iota(jnp.int32, shape, 0), t)
    j = lax.broadcasted_iota(jnp.int32, shape, 1)
    valid = (j > r) & (j <= r + t) & ((j >= t) | (i > 0))
    for kvh in range(kv_heads):
        qg = _stack_group(q, kvh, group, hd).astype(BF16)
        s = _dot_nt(qg, kk[:, kvh * hd:(kvh + 1) * hd]) * scale
        o = _sink_attend(s, valid, _sink_rows(sinks_ref, kvh, group, t), vv[:, kvh * hd:(kvh + 1) * hd])
        for g in range(group):
            hh = kvh * group + g
            o_scr[:, hh * hd:(hh + 1) * hd] = o[g * t:(g + 1) * t]
    _swa_finish(o_scr, x_ref, wo_ref, bo_ref, y_ref)


def _swa_prompt_attn(q, kv, x, sinks, w_o, b_o, kv_heads, hd, scale):
    m, d = x.shape
    qw = q.shape[1]
    kw = kv_heads * hd
    t = Q_BLOCK
    group = qw // kw
    prev = lambda col: pl.BlockSpec((t, kw), lambda i: (jnp.maximum(i - 1, 0), col))
    return pl.pallas_call(
        functools.partial(_swa_prompt_kernel, kv_heads=kv_heads, group=group, hd=hd, scale=scale),
        grid=(m // t,),
        in_specs=[pl.BlockSpec(memory_space=pltpu.SMEM),
                  pl.BlockSpec((t, qw), lambda i: (i, 0)),
                  pl.BlockSpec((t, 2 * kw), lambda i: (i, 0)),
                  prev(0), prev(1),
                  pl.BlockSpec((t, d), lambda i: (i, 0)),
                  _const_spec(w_o.shape), _const_spec((1, d))],
        out_specs=pl.BlockSpec((t, d), lambda i: (i, 0)),
        out_shape=jax.ShapeDtypeStruct((m, d), F32),
        scratch_shapes=[pltpu.VMEM((t, qw), F32)],
        compiler_params=_cparams(1),
        name="swa_prompt_attn",
    )(sinks, q, kv, kv, kv, x, w_o, b_o.reshape(1, d))


def _swa_sample_kernel(sinks_ref, q_ref, new_ref, ck_ref, cv_ref, x_ref, wo_ref, bo_ref, y_ref, o_scr,
                       *, kv_heads, group, hd, dec_seq, scale):
    n_seq, wb, kw = ck_ref.shape
    t = n_seq * dec_seq
    q = q_ref[...]
    kk = jnp.concatenate([ck_ref[...].reshape(n_seq * wb, kw), new_ref[:, :kw]], axis=0).astype(BF16)
    vv = jnp.concatenate([cv_ref[...].reshape(n_seq * wb, kw), new_ref[:, kw:]], axis=0).astype(BF16)
    n_old = n_seq * wb
    shape = (group * t, n_old + t)
    r = _irem(lax.broadcasted_iota(jnp.int32, shape, 0), t)
    r_seq, r_tok = _idiv(r, dec_seq), _irem(r, dec_seq)
    j = lax.broadcasted_iota(jnp.int32, shape, 1)
    old = j < n_old
    j_new = jnp.maximum(j - n_old, 0)
    j_seq = jnp.where(old, _idiv(j, wb), _idiv(j_new, dec_seq))
    j_pos = jnp.where(old, _irem(j, wb), wb + _irem(j_new, dec_seq))
    valid = (j_seq == r_seq) & (j_pos > r_tok) & (j_pos <= r_tok + wb)
    for kvh in range(kv_heads):
        qg = _stack_group(q, kvh, group, hd).astype(BF16)
        s = _dot_nt(qg, kk[:, kvh * hd:(kvh + 1) * hd]) * scale
        o = _sink_attend(s, valid, _sink_rows(sinks_ref, kvh, group, t), vv[:, kvh * hd:(kvh + 1) * hd])
        for g in range(group):
            hh = kvh * group + g
            o_scr[:, hh * hd:(hh + 1) * hd] = o[g * t:(g + 1) * t]
    _swa_finish(o_scr, x_ref, wo_ref, bo_ref, y_ref)


def _swa_sample_attn(q, kv_new, cache_k, cache_v, x, sinks, w_o, b_o, kv_heads, hd, dec_seq, scale, n_seq=8):
    m, d = x.shape
    qw = q.shape[1]
    kw = kv_heads * hd
    b, wb = cache_k.shape[:2]
    n_seq = min(n_seq, b)
    t = n_seq * dec_seq
    group = qw // kw
    return pl.pallas_call(
        functools.partial(_swa_sample_kernel, kv_heads=kv_heads, group=group, hd=hd, dec_seq=dec_seq, scale=scale),
        grid=(b // n_seq,),
        in_specs=[pl.BlockSpec(memory_space=pltpu.SMEM),
                  pl.BlockSpec((t, qw), lambda i: (i, 0)),
                  pl.BlockSpec((t, 2 * kw), lambda i: (i, 0)),
                  pl.BlockSpec((n_seq, wb, kw), lambda i: (i, 0, 0)),
                  pl.BlockSpec((n_seq, wb, kw), lambda i: (i, 0, 0)),
                  pl.BlockSpec((t, d), lambda i: (i, 0)),
                  _const_spec(w_o.shape), _const_spec((1, d))],
        out_specs=pl.BlockSpec((t, d), lambda i: (i, 0)),
        out_shape=jax.ShapeDtypeStruct((m, d), F32),
        scratch_shapes=[pltpu.VMEM((t, qw), F32)],
        compiler_params=_cparams(1),
        name="swa_sample_attn",
    )(sinks, q, kv_new, cache_k.reshape(b, wb, kw), cache_v.reshape(b, wb, kw), x, w_o, b_o.reshape(1, d))


def _prep_weights(p):
    n_a, kv_lora, heads, nope = p["mla_w_uk"].shape
    q_lora = p["mla_w_dq"].shape[2]
    d = p["mla_w_o"].shape[2]
    v_dim = p["mla_w_uv"].shape[3]
    w = {"ffn_in": p["ffn_w_in"].astype(BF16), "ffn_out": p["ffn_w_out"].astype(BF16), "mla": []}
    for l in range(n_a):
        w_uq = p["mla_w_uq"][l].reshape(q_lora, heads, nope + ROPE_DIM)
        w_dkv = p["mla_w_dkv"][l]
        w["mla"].append({
            "norm": p["mla_norm"][l], "q_norm": p["mla_q_norm"][l], "kv_norm": p["mla_kv_norm"][l],
            "w_dq": p["mla_w_dq"][l].astype(BF16),
            "w_uq_n": w_uq[:, :, :nope].reshape(q_lora, heads * nope).astype(BF16),
            "w_uq_r": w_uq[:, :, nope:].reshape(q_lora, heads * ROPE_DIM).astype(BF16),
            "w_uk": jnp.transpose(p["mla_w_uk"][l], (1, 2, 0)).astype(BF16),
            "w_dkv_c": w_dkv[:, :kv_lora].astype(BF16),
            "w_dkv_r": jnp.pad(w_dkv[:, kv_lora:], ((0, 0), (0, LANES - ROPE_DIM))).astype(BF16),
            "w_uv": jnp.transpose(p["mla_w_uv"][l], (1, 0, 2)).astype(BF16),
            "w_o": p["mla_w_o"][l].reshape(heads, v_dim, d).astype(BF16),
        })
    w["swa_kv"] = p["swa_w_kv"].astype(BF16)
    w["swa_q"] = p["swa_w_q"].astype(BF16)
    w["swa_o"] = p["swa_w_o"].astype(BF16)
    return w


def _run_trunk(x, pos, p, w, mla_q_scale, mla_attend, swa_attend):
    depth = p["ffn_norm"].shape[0]
    n_a = p["mla_norm"].shape[0]
    kv_w = w["swa_kv"].shape[1]
    cos_t, sin_t = _rope_tables(pos)
    rows_all, kv = [], None
    for l in range(depth):
        if l == n_a:
            kv = _norm_linear(x, p["kv_norm"], w["swa_kv"], p["swa_b_kv"], cos_t, sin_t, rope_cols=kv_w // 2)
        x = _ffn(x, p["ffn_norm"][l, 0], w["ffn_in"][l, 0], w["ffn_out"][l, 0])
        if l < n_a:
            wl = w["mla"][l]
            q, rows, rows_bf = _mla_proj(x, wl, cos_t, sin_t, mla_q_scale)
            o_lat = mla_attend(l, q, rows, rows_bf)
            x = _mla_out(o_lat, x, wl["w_uv"], wl["w_o"])
            rows_all.append(rows)
        else:
            jl = l - n_a
            q = _norm_linear(x, p["swa_norm"][jl], w["swa_q"][jl], p["swa_b_q"][jl], cos_t, sin_t,
                             rope_cols=w["swa_q"].shape[2])
            x = swa_attend(q, kv, x, p["swa_sinks"][jl], w["swa_o"][jl], p["swa_b_o"][jl])
        x = _ffn(x, p["ffn_norm"][l, 1], w["ffn_in"][l, 1], w["ffn_out"][l, 1],
                 final_g=p["final_norm"] if l == depth - 1 else None)
    return x, jnp.stack(rows_all, axis=0), kv


def kernel(x_prompt, x_sample, cache_mla, cache_swa_k, cache_swa_v, page_table, ffn_norm, ffn_w_in, ffn_w_out, mla_norm, mla_w_dq, mla_q_norm, mla_w_uq, mla_w_dkv, mla_kv_norm, mla_w_uk, mla_w_uv, mla_w_o, kv_norm, swa_w_kv, swa_b_kv, swa_norm, swa_w_q, swa_b_q, swa_sinks, swa_w_o, swa_b_o, final_norm):
    p = {
        "ffn_norm": ffn_norm, "ffn_w_in": ffn_w_in, "ffn_w_out": ffn_w_out,
        "mla_norm": mla_norm, "mla_w_dq": mla_w_dq, "mla_q_norm": mla_q_norm, "mla_w_uq": mla_w_uq,
        "mla_w_dkv": mla_w_dkv, "mla_kv_norm": mla_kv_norm, "mla_w_uk": mla_w_uk, "mla_w_uv": mla_w_uv,
        "mla_w_o": mla_w_o,
        "kv_norm": kv_norm, "swa_w_kv": swa_w_kv, "swa_b_kv": swa_b_kv,
        "swa_norm": swa_norm, "swa_w_q": swa_w_q, "swa_b_q": swa_b_q, "swa_sinks": swa_sinks,
        "swa_w_o": swa_w_o, "swa_b_o": swa_b_o, "final_norm": final_norm,
    }
    batch, seq, d = x_prompt.shape
    dec_batch, dec_seq, _ = x_sample.shape
    _, kv_lora, heads, nope = mla_w_uk.shape
    e = kv_lora + ROPE_DIM
    kv_heads, hd = cache_swa_k.shape[2:]
    past_len = page_table.shape[1] * cache_mla.shape[2]
    mla_q_scale = float(nope + ROPE_DIM) ** -0.5 * LOG2_E
    swa_scale = float(hd) ** -0.5
    cache_t = jnp.swapaxes(cache_mla, 2, 3)
    assert batch == 1 and seq % Q_BLOCK == 0 and cache_swa_k.shape[1] == Q_BLOCK
    w = _prep_weights(p)

    def mla_prompt(l, q, rows, rows_bf):
        return _mla_prompt_attn(q, rows_bf, kv_lora)

    def swa_prompt(q, kv, x, sinks, w_o, b_o):
        return _swa_prompt_attn(q, kv, x, sinks, w_o, b_o, kv_heads, hd, swa_scale)

    y_p, rows_p, kv_p = _run_trunk(x_prompt.reshape(seq, d), jnp.arange(seq, dtype=jnp.int32), p, w, mla_q_scale,
                                   mla_prompt, swa_prompt)

    n_tok = dec_batch * dec_seq

    def mla_sample(l, q, rows, rows_bf):
        q_tok = jnp.transpose(q, (0, 2, 1, 3)).reshape(dec_batch, dec_seq, heads, e)
        q_seq = jnp.transpose(q_tok, (0, 2, 1, 3)).reshape(dec_batch, heads * dec_seq, e)
        o = _mla_sample_attn(q_seq, rows.reshape(dec_batch, dec_seq, e), cache_t, l, page_table, kv_lora)
        o = jnp.transpose(o.reshape(dec_batch, heads, dec_seq, kv_lora), (1, 0, 2, 3))
        return o.reshape(1, heads, n_tok, kv_lora)

    def swa_sample(q, kv, x, sinks, w_o, b_o):
        return _swa_sample_attn(q, kv, cache_swa_k, cache_swa_v, x, sinks, w_o, b_o, kv_heads, hd, dec_seq, swa_scale)

    pos_s = jnp.tile(past_len + jnp.arange(dec_seq, dtype=jnp.int32), dec_batch)
    y_s, rows_s, kv_s = _run_trunk(x_sample.reshape(n_tok, d), pos_s, p, w, mla_q_scale, mla_sample, swa_sample)

    kw = kv_heads * hd
    w_p = min(Q_BLOCK, seq)
    k_p = kv_p[seq - w_p:, :kw].reshape(1, w_p, kv_heads, hd)
    v_p = kv_p[seq - w_p:, kw:].reshape(1, w_p, kv_heads, hd)
    k_s = kv_s[:, :kw].reshape(dec_batch, dec_seq, kv_heads, hd)
    v_s = kv_s[:, kw:].reshape(dec_batch, dec_seq, kv_heads, hd)
    swa_k_sample = jnp.concatenate([cache_swa_k, k_s], axis=1)[:, dec_seq:]
    swa_v_sample = jnp.concatenate([cache_swa_v, v_s], axis=1)[:, dec_seq:]
    return (y_p.reshape(1, seq, d), y_s.reshape(dec_batch, dec_seq, d),
            rows_p.reshape(-1, 1, seq, e), rows_s.reshape(-1, dec_batch, dec_seq, e),
            k_p, v_p, swa_k_sample, swa_v_sample)
```

```python
import functools

import jax
import jax.numpy as jnp
from jax import lax
from jax.experimental import pallas as pl
from jax.experimental.pallas import tpu as pltpu

F32 = jnp.float32
BF16 = jnp.bfloat16

EPS = 1e-6
ROPE_THETA = 10000.0
FFN_RES_WEIGHT = 0.5
LANES = 128
ROPE_DIM = 64
Q_BLOCK = 128
VMEM_LIMIT = 56 * 1024 * 1024
NEG_INF = float("-inf")
LOG2_E = 1.4426950408889634


def _cparams(n_axes, vmem=None):
    return pltpu.CompilerParams(dimension_semantics=("arbitrary",) * n_axes, vmem_limit_bytes=vmem)


def _const_spec(shape):
    nd = len(shape)
    return pl.BlockSpec(shape, lambda *_: (0,) * nd, pipeline_mode=pl.Buffered(1))


def _dot(a, b):
    return jnp.dot(a, b, preferred_element_type=F32)


def _dot_nt(a, b):
    return lax.dot_general(a, b, (((1,), (1,)), ((), ())), preferred_element_type=F32)


def _idiv(x, n):
    if n & (n - 1) == 0:
        return x >> (n.bit_length() - 1)
    return lax.div(x, jnp.full(x.shape, n, x.dtype))


def _irem(x, n):
    if n & (n - 1) == 0:
        return x & (n - 1)
    return lax.rem(x, jnp.full(x.shape, n, x.dtype))


def _rms(x, g):
    return x * lax.rsqrt(jnp.mean(x * x, axis=-1, keepdims=True) + EPS) * g


def _rope128(x, cos_t, sin_t):
    lane = lax.broadcasted_iota(jnp.int32, x.shape, 1)
    first_half = (lane & (ROPE_DIM // 2)) == 0
    partner = jnp.where(first_half, pltpu.roll(x, LANES - ROPE_DIM // 2, 1), pltpu.roll(x, ROPE_DIM // 2, 1))
    return x * cos_t + partner * sin_t


def _rope_tables(pos):
    half = ROPE_DIM // 2
    inv_freq = jnp.exp(-(jnp.arange(half, dtype=F32) / half) * jnp.log(F32(ROPE_THETA)))
    ang = pos.astype(F32)[:, None] * inv_freq[None, :]
    cos, sin = jnp.cos(ang), jnp.sin(ang)
    reps = LANES // ROPE_DIM
    cos_t = jnp.tile(cos, (1, 2 * reps))
    sin_t = jnp.tile(jnp.concatenate([-sin, sin], axis=1), (1, reps))
    return cos_t, sin_t


def _ffn_kernel(x_ref, g_ref, win_ref, wout_ref, *rest, d_ff, n_chunks, final):
    if final:
        gf_ref, o_ref = rest
    else:
        (o_ref,) = rest
    x = x_ref[...]
    h = _rms(x, g_ref[...]).astype(BF16)
    ck = d_ff // n_chunks
    acc = None
    for c in range(n_chunks):
        gate = _dot(h, win_ref[:, c * ck:(c + 1) * ck])
        up = _dot(h, win_ref[:, d_ff + c * ck:d_ff + (c + 1) * ck])
        act = (gate * (1.0 / (1.0 + jnp.exp(-gate))) * up).astype(BF16)
        part = _dot(act, wout_ref[c * ck:(c + 1) * ck, :])
        acc = part if acc is None else acc + part
    y = x + FFN_RES_WEIGHT * acc
    if final:
        y = _rms(y, gf_ref[...])
    o_ref[...] = y


def _ffn(x, g, w_in, w_out, final_g=None, tm=512):
    m, d = x.shape
    d_ff = w_out.shape[0]
    tm = min(tm, m)
    n_chunks = 2 if d_ff % (2 * LANES) == 0 else 1
    in_specs = [
        pl.BlockSpec((tm, d), lambda i: (i, 0)),
        _const_spec((1, d)),
        _const_spec((d, 2 * d_ff)),
        _const_spec((d_ff, d)),
    ]
    args = [x, g.reshape(1, d), w_in, w_out]
    if final_g is not None:
        in_specs.append(_const_spec((1, d)))
        args.append(final_g.reshape(1, d))
    return pl.pallas_call(
        functools.partial(_ffn_kernel, d_ff=d_ff, n_chunks=n_chunks, final=final_g is not None),
        grid=(m // tm,),
        in_specs=in_specs,
        out_specs=pl.BlockSpec((tm, d), lambda i: (i, 0)),
        out_shape=jax.ShapeDtypeStruct((m, d), F32),
        compiler_params=_cparams(1, VMEM_LIMIT),
        name="ffn_half",
    )(*args)


def _mla_proj_kernel(x_ref, g_ref, wdq_ref, gq_ref, wuqn_ref, wuqr_ref, wuk_ref, wdkvc_ref, wdkvr_ref,
                     gkv_ref, cos_ref, sin_ref, q_ref, rows_ref, rowsb_ref, *, heads, nope, kv_lora, q_scale):
    tm = x_ref.shape[0]
    cos_t, sin_t = cos_ref[...], sin_ref[...]
    h = _rms(x_ref[...], g_ref[...]).astype(BF16)
    cq = _rms(_dot(h, wdq_ref[...]), gq_ref[...]).astype(BF16)
    qn = _dot(cq, wuqn_ref[...])
    qr = _dot(cq, wuqr_ref[...])
    qr = jnp.concatenate(
        [_rope128(qr[:, c * LANES:(c + 1) * LANES], cos_t, sin_t) for c in range(qr.shape[1] // LANES)], axis=1)
    c_lat = _rms(_dot(h, wdkvc_ref[...]), gkv_ref[...])
    k_pe = _rope128(_dot(h, wdkvr_ref[...]), cos_t, sin_t)[:, :ROPE_DIM]
    rows_ref[:, :kv_lora] = c_lat
    rows_ref[:, kv_lora:] = k_pe
    rowsb_ref[:, :kv_lora] = c_lat.astype(BF16)
    rowsb_ref[:, kv_lora:] = k_pe.astype(BF16)
    for hd in range(heads):
        q_lat = (_dot(qn[:, hd * nope:(hd + 1) * nope].astype(BF16), wuk_ref[hd]) * q_scale).astype(BF16)
        q_pe = (qr[:, hd * ROPE_DIM:(hd + 1) * ROPE_DIM] * q_scale).astype(BF16)
        for r in range(tm // Q_BLOCK):
            q_ref[r, hd, :, :kv_lora] = q_lat[r * Q_BLOCK:(r + 1) * Q_BLOCK]
            q_ref[r, hd, :, kv_lora:] = q_pe[r * Q_BLOCK:(r + 1) * Q_BLOCK]


def _mla_proj(x, w, cos_t, sin_t, q_scale, tm=256):
    m, d = x.shape
    heads, nope, kv_lora = w["w_uk"].shape
    e = kv_lora + ROPE_DIM
    q_lora = w["w_dq"].shape[1]
    tm = min(tm, m)
    nb = tm // Q_BLOCK
    consts = [w["norm"].reshape(1, d), w["w_dq"], w["q_norm"].reshape(1, q_lora), w["w_uq_n"], w["w_uq_r"],
              w["w_uk"], w["w_dkv_c"], w["w_dkv_r"], w["kv_norm"].reshape(1, kv_lora)]
    row_spec = lambda width: pl.BlockSpec((tm, width), lambda i: (i, 0))
    return pl.pallas_call(
        functools.partial(_mla_proj_kernel, heads=heads, nope=nope, kv_lora=kv_lora, q_scale=q_scale),
        grid=(m // tm,),
        in_specs=[row_spec(d)] + [_const_spec(c.shape) for c in consts] + [row_spec(LANES), row_spec(LANES)],
        out_specs=[pl.BlockSpec((nb, heads, Q_BLOCK, e), lambda i: (i, 0, 0, 0)), row_spec(e), row_spec(e)],
        out_shape=[jax.ShapeDtypeStruct((m // Q_BLOCK, heads, Q_BLOCK, e), BF16),
                   jax.ShapeDtypeStruct((m, e), F32),
                   jax.ShapeDtypeStruct((m, e), BF16)],
        compiler_params=_cparams(1),
        name="mla_proj",
    )(x, *consts, cos_t, sin_t)


def _flash_scores(q_ref, k_ref, s_scr, slot, start, *, tk):
    heads, _, e = q_ref.shape[1:]
    s_scr[slot] = _dot_nt(q_ref[0].reshape(heads * Q_BLOCK, e), k_ref[pl.ds(start, tk), :])


def _flash_update(k_ref, s_scr, slot, start, q_pos0, m_ref, l_ref, acc_ref, *, tk, kv_lora):
    s = s_scr[slot]
    if q_pos0 is not None:
        q_pos = q_pos0 + (lax.broadcasted_iota(jnp.int32, s.shape, 0) & (Q_BLOCK - 1))
        k_pos = start + lax.broadcasted_iota(jnp.int32, s.shape, 1)
        s = jnp.where(k_pos <= q_pos, s, NEG_INF)
    chunks = [s[:, j * LANES:(j + 1) * LANES] for j in range(tk // LANES)]
    m_prev = m_ref[...]
    m_new = jnp.maximum(m_prev, jnp.max(functools.reduce(jnp.maximum, chunks), axis=-1, keepdims=True))
    alpha = jnp.exp2(m_prev - m_new)
    ps = [jnp.exp2(ch - m_new) for ch in chunks]
    l_ref[...] = alpha * l_ref[...] + functools.reduce(jnp.add, ps)
    pv = _dot(jnp.concatenate(ps, axis=1).astype(BF16), k_ref[pl.ds(start, tk), :kv_lora])
    acc_ref[...] = acc_ref[...] * jnp.concatenate([alpha] * (kv_lora // LANES), axis=1) + pv
    m_ref[...] = m_new


def _mla_prompt_kernel(q_ref, k_ref, o_ref, s_scr, m_ref, l_ref, acc_ref, *, tk, kv_lora):
    i = pl.program_id(0)
    heads = q_ref.shape[1]
    m_ref[...] = jnp.full(m_ref.shape, NEG_INF, F32)
    l_ref[...] = jnp.zeros(l_ref.shape, F32)
    acc_ref[...] = jnp.zeros(acc_ref.shape, F32)
    scores = functools.partial(_flash_scores, q_ref, k_ref, s_scr, tk=tk)
    update = functools.partial(_flash_update, k_ref, s_scr, m_ref=m_ref, l_ref=l_ref, acc_ref=acc_ref,
                               tk=tk, kv_lora=kv_lora)
    at = lambda j: pl.multiple_of(j * tk, tk)
    n_full = (i * Q_BLOCK) // tk
    n_pairs = n_full // 2
    scores(0, at(0))

    def body(p, carry):
        j = 2 * p
        scores(1, at(j + 1))
        update(0, at(j), None)
        scores(0, at(j + 2))
        update(1, at(j + 1), None)
        return carry

    lax.fori_loop(0, n_pairs, body, 0)
    j0 = 2 * n_pairs
    q_pos0 = i * Q_BLOCK

    @pl.when(n_full == j0)
    def _():
        update(0, at(j0), q_pos0)

    @pl.when(n_full != j0)
    def _():
        scores(1, at(j0 + 1))
        update(0, at(j0), None)
        update(1, at(j0 + 1), q_pos0)

    l_tot = jnp.sum(l_ref[...], axis=-1, keepdims=True)
    o_ref[0] = (acc_ref[...] / l_tot).reshape(heads, Q_BLOCK, kv_lora)


def _mla_prompt_attn(q, rows_bf, kv_lora, tk=1024):
    nb, heads, _, e = q.shape
    s = rows_bf.shape[0]
    tk = min(tk, s)
    assert s % tk == 0 and tk % Q_BLOCK == 0
    rows = heads * Q_BLOCK
    return pl.pallas_call(
        functools.partial(_mla_prompt_kernel, tk=tk, kv_lora=kv_lora),
        grid=(nb,),
        in_specs=[pl.BlockSpec((1, heads, Q_BLOCK, e), lambda i: (i, 0, 0, 0)), _const_spec((s, e))],
        out_specs=pl.BlockSpec((1, heads, Q_BLOCK, kv_lora), lambda i: (i, 0, 0, 0)),
        out_shape=jax.ShapeDtypeStruct((nb, heads, Q_BLOCK, kv_lora), F32),
        scratch_shapes=[pltpu.VMEM((2, rows, tk), F32), pltpu.VMEM((rows, LANES), F32),
                        pltpu.VMEM((rows, LANES), F32), pltpu.VMEM((rows, kv_lora), F32)],
        compiler_params=_cparams(1, VMEM_LIMIT),
        name="mla_prompt_attn",
    )(q, rows_bf)


def _mla_sample_kernel(pt_ref, q_ref, new_ref, cache_ref, o_ref, pbuf, sem, kbuf_ref, nbuf_ref, m_ref, l_ref, acc_ref,
                       *, layer, n_group, dec_seq, kv_lora):
    p_idx = pl.program_id(1)
    n_steps = pl.num_programs(0) * pl.num_programs(1)
    t = pl.program_id(0) * pl.num_programs(1) + p_idx
    slot = t & 1
    page = pbuf.shape[3]

    def page_copies(step, sl):
        return [pltpu.make_async_copy(cache_ref.at[layer, pt_ref[step * n_group + g]], pbuf.at[sl, g], sem.at[sl])
                for g in range(n_group)]

    @pl.when(t == 0)
    def _():
        for c in page_copies(0, 0):
            c.start()

    @pl.when(t + 1 < n_steps)
    def _():
        for c in page_copies(t + 1, 1 - slot):
            c.start()

    q = q_ref[...]

    @pl.when(p_idx == 0)
    def _():
        nbuf_ref[...] = jnp.zeros(nbuf_ref.shape, F32)
        nbuf_ref[:dec_seq, :] = new_ref[...]
        nk = nbuf_ref[...].astype(BF16)
        s = _dot_nt(q, nk)
        tok = _irem(lax.broadcasted_iota(jnp.int32, s.shape, 0), dec_seq)
        s = jnp.where(lax.broadcasted_iota(jnp.int32, s.shape, 1) <= tok, s, NEG_INF)
        m0 = jnp.max(s, axis=-1, keepdims=True)
        p0 = jnp.exp2(s - m0)
        m_ref[...] = m0
        l_ref[...] = jnp.sum(p0, axis=-1, keepdims=True)
        acc_ref[...] = _dot(p0.astype(BF16), nk[:, :kv_lora])

    for c in page_copies(t, slot):
        c.wait()
    for g in range(n_group):
        kbuf_ref[:, g * page:(g + 1) * page] = pbuf[slot, g].astype(BF16)
    kt = kbuf_ref[...]
    s = _dot(q, kt)
    m_prev = m_ref[...]
    m_new = jnp.maximum(m_prev, jnp.max(s, axis=-1, keepdims=True))
    alpha = jnp.exp2(m_prev - m_new)
    p = jnp.exp2(s - m_new)
    l_ref[...] = alpha * l_ref[...] + jnp.sum(p, axis=-1, keepdims=True)
    acc_ref[...] = alpha * acc_ref[...] + _dot_nt(p.astype(BF16), kt[:kv_lora, :])
    m_ref[...] = m_new

    @pl.when(p_idx == pl.num_programs(1) - 1)
    def _():
        o_ref[...] = acc_ref[...] / l_ref[...]


def _mla_sample_attn(q, new_rows, cache_t, layer, page_table, kv_lora, n_group=32):
    b, qrows, e = q.shape
    dec_seq = new_rows.shape[1]
    n_pages = page_table.shape[1]
    page = cache_t.shape[3]
    n_group = min(n_group, n_pages)
    assert n_pages % n_group == 0 and dec_seq <= page

    grid_spec = pltpu.PrefetchScalarGridSpec(
        num_scalar_prefetch=1,
        grid=(b, n_pages // n_group),
        in_specs=[pl.BlockSpec((None, qrows, e), lambda bi, pi, pt: (bi, 0, 0)),
                  pl.BlockSpec((None, dec_seq, e), lambda bi, pi, pt: (bi, 0, 0)),
                  pl.BlockSpec(memory_space=pl.ANY)],
        out_specs=pl.BlockSpec((None, qrows, kv_lora), lambda bi, pi, pt: (bi, 0, 0)),
        scratch_shapes=[pltpu.VMEM((2, n_group, e, page), F32), pltpu.SemaphoreType.DMA((2,)),
                        pltpu.VMEM((e, n_group * page), BF16), pltpu.VMEM((page, e), F32),
                        pltpu.VMEM((qrows, 1), F32), pltpu.VMEM((qrows, 1), F32), pltpu.VMEM((qrows, kv_lora), F32)],
    )
    return pl.pallas_call(
        functools.partial(_mla_sample_kernel, layer=layer, n_group=n_group, dec_seq=dec_seq, kv_lora=kv_lora),
        grid_spec=grid_spec,
        out_shape=jax.ShapeDtypeStruct((b, qrows, kv_lora), F32),
        compiler_params=_cparams(2),
        name="mla_sample_attn",
    )(page_table.reshape(-1), q, new_rows, cache_t)


def _mla_out_kernel(o_ref, x_ref, wuv_ref, wo_ref, y_ref):
    nblk, heads, tq, kv_lora = o_ref.shape
    acc = x_ref[...]
    for hd in range(heads):
        o_h = _dot(o_ref[:, hd].reshape(nblk * tq, kv_lora).astype(BF16), wuv_ref[hd]).astype(BF16)
        acc = acc + _dot(o_h, wo_ref[hd])
    y_ref[...] = acc


def _mla_out(o_lat, x, w_uv, w_o, rows_per_step=512):
    nb, heads, tq, kv_lora = o_lat.shape
    d = x.shape[1]
    nblk = max(1, min(nb, rows_per_step // tq))
    assert nb % nblk == 0
    return pl.pallas_call(
        _mla_out_kernel,
        grid=(nb // nblk,),
        in_specs=[pl.BlockSpec((nblk, heads, tq, kv_lora), lambda i: (i, 0, 0, 0)),
                  pl.BlockSpec((nblk * tq, d), lambda i: (i, 0)),
                  _const_spec(w_uv.shape), _const_spec(w_o.shape)],
        out_specs=pl.BlockSpec((nblk * tq, d), lambda i: (i, 0)),
        out_shape=jax.ShapeDtypeStruct(x.shape, F32),
        compiler_params=_cparams(1),
        name="mla_out",
    )(o_lat, x, w_uv, w_o)


def _norm_linear_kernel(x_ref, g_ref, w_ref, b_ref, cos_ref, sin_ref, y_ref, *, rope_cols):
    h = _rms(x_ref[...], g_ref[...]).astype(BF16)
    y = _dot(h, w_ref[...]) + b_ref[...]
    cos_t, sin_t = cos_ref[...], sin_ref[...]
    for c in range(rope_cols // LANES):
        y_ref[:, c * LANES:(c + 1) * LANES] = _rope128(y[:, c * LANES:(c + 1) * LANES], cos_t, sin_t)
    if rope_cols < y.shape[1]:
        y_ref[:, rope_cols:] = y[:, rope_cols:]


def _norm_linear(x, g, w, b, cos_t, sin_t, rope_cols, tm=512):
    m, d = x.shape
    n = w.shape[1]
    tm = min(tm, m)
    row_spec = lambda width: pl.BlockSpec((tm, width), lambda i: (i, 0))
    return pl.pallas_call(
        functools.partial(_norm_linear_kernel, rope_cols=rope_cols),
        grid=(m // tm,),
        in_specs=[row_spec(d), _const_spec((1, d)), _const_spec((d, n)), _const_spec((1, n)),
                  row_spec(LANES), row_spec(LANES)],
        out_specs=row_spec(n),
        out_shape=jax.ShapeDtypeStruct((m, n), F32),
        compiler_params=_cparams(1),
        name="norm_linear_rope",
    )(x, g.reshape(1, d), w, b.reshape(1, n), cos_t, sin_t)


def _sink_attend(s, valid, sink, v):
    s = jnp.where(valid, s, NEG_INF)
    m = jnp.maximum(jnp.max(s, axis=-1, keepdims=True), sink)
    p = jnp.exp(s - m)
    pr = p / (jnp.sum(p, axis=-1, keepdims=True) + jnp.exp(sink - m))
    return _dot(pr.astype(BF16), v)


def _stack_group(q, kvh, group, hd):
    return jnp.concatenate([q[:, (kvh * group + g) * hd:(kvh * group + g + 1) * hd] for g in range(group)], axis=0)


def _sink_rows(sinks_ref, kvh, group, t):
    row = lax.broadcasted_iota(jnp.int32, (group * t, 1), 0)
    sink = jnp.full((group * t, 1), sinks_ref[kvh * group], F32)
    for g in range(1, group):
        sink = jnp.where(row >= g * t, sinks_ref[kvh * group + g], sink)
    return sink


def _swa_finish(o_scr, x_ref, wo_ref, bo_ref, y_ref):
    y_ref[...] = x_ref[...] + _dot(o_scr[...].astype(BF16), wo_ref[...]) + bo_ref[...]


def _swa_prompt_kernel(sinks_ref, q_ref, cur_ref, pk_ref, pv_ref, x_ref, wo_ref, bo_ref, y_ref, o_scr,
                       *, kv_heads, group, hd, scale):
    i = pl.program_id(0)
    t = q_ref.shape[0]
    kw = kv_heads * hd
    q = q_ref[...]
    kk = jnp.concatenate([pk_ref[...], cur_ref[:, :kw]], axis=0).astype(BF16)
    vv = jnp.concatenate([pv_ref[...], cur_ref[:, kw:]], axis=0).astype(BF16)
    shape = (group * t, 2 * t)
    r = _irem(lax.broadcasted_iota(jnp.int32, shape, 0), t)
    j = lax.broadcasted_iota(jnp.int32, shape, 1)
    valid = (j > r) & (j <= r + t) & ((j >= t) | (i > 0))
    for kvh in range(kv_heads):
        qg = _stack_group(q, kvh, group, hd).astype(BF16)
        s = _dot_nt(qg, kk[:, kvh * hd:(kvh + 1) * hd]) * scale
        o = _sink_attend(s, valid, _sink_rows(sinks_ref, kvh, group, t), vv[:, kvh * hd:(kvh + 1) * hd])
        for g in range(group):
            hh = kvh * group + g
            o_scr[:, hh * hd:(hh + 1) * hd] = o[g * t:(g + 1) * t]
    _swa_finish(o_scr, x_ref, wo_ref, bo_ref, y_ref)


def _swa_prompt_attn(q, kv, x, sinks, w_o, b_o, kv_heads, hd, scale):
    m, d = x.shape
    qw = q.shape[1]
    kw = kv_heads * hd
    t = Q_BLOCK
    group = qw // kw
    prev = lambda col: pl.BlockSpec((t, kw), lambda i: (jnp.maximum(i - 1, 0), col))
    return pl.pallas_call(
        functools.partial(_swa_prompt_kernel, kv_heads=kv_heads, group=group, hd=hd, scale=scale),
        grid=(m // t,),
        in_specs=[pl.BlockSpec(memory_space=pltpu.SMEM),
                  pl.BlockSpec((t, qw), lambda i: (i, 0)),
                  pl.BlockSpec((t, 2 * kw), lambda i: (i, 0)),
                  prev(0), prev(1),
                  pl.BlockSpec((t, d), lambda i: (i, 0)),
                  _const_spec(w_o.shape), _const_spec((1, d))],
        out_specs=pl.BlockSpec((t, d), lambda i: (i, 0)),
        out_shape=jax.ShapeDtypeStruct((m, d), F32),
        scratch_shapes=[pltpu.VMEM((t, qw), F32)],
        compiler_params=_cparams(1),
        name="swa_prompt_attn",
    )(sinks, q, kv, kv, kv, x, w_o, b_o.reshape(1, d))


def _swa_sample_kernel(sinks_ref, q_ref, new_ref, ck_ref, cv_ref, x_ref, wo_ref, bo_ref, y_ref, o_scr,
                       *, kv_heads, group, hd, dec_seq, scale):
    n_seq, wb, kw = ck_ref.shape
    t = n_seq * dec_seq
    q = q_ref[...]
    kk = jnp.concatenate([ck_ref[...].reshape(n_seq * wb, kw), new_ref[:, :kw]], axis=0).astype(BF16)
    vv = jnp.concatenate([cv_ref[...].reshape(n_seq * wb, kw), new_ref[:, kw:]], axis=0).astype(BF16)
    n_old = n_seq * wb
    shape = (group * t, n_old + t)
    r = _irem(lax.broadcasted_iota(jnp.int32, shape, 0), t)
    r_seq, r_tok = _idiv(r, dec_seq), _irem(r, dec_seq)
    j = lax.broadcasted_iota(jnp.int32, shape, 1)
    old = j < n_old
    j_new = jnp.maximum(j - n_old, 0)
    j_seq = jnp.where(old, _idiv(j, wb), _idiv(j_new, dec_seq))
    j_pos = jnp.where(old, _irem(j, wb), wb + _irem(j_new, dec_seq))
    valid = (j_seq == r_seq) & (j_pos > r_tok) & (j_pos <= r_tok + wb)
    for kvh in range(kv_heads):
        qg = _stack_group(q, kvh, group, hd).astype(BF16)
        s = _dot_nt(qg, kk[:, kvh * hd:(kvh + 1) * hd]) * scale
        o = _sink_attend(s, valid, _sink_rows(sinks_ref, kvh, group, t), vv[:, kvh * hd:(kvh + 1) * hd])
        for g in range(group):
            hh = kvh * group + g
            o_scr[:, hh * hd:(hh + 1) * hd] = o[g * t:(g + 1) * t]
    _swa_finish(o_scr, x_ref, wo_ref, bo_ref, y_ref)


def _swa_sample_attn(q, kv_new, cache_k, cache_v, x, sinks, w_o, b_o, kv_heads, hd, dec_seq, scale, n_seq=8):
    m, d = x.shape
    qw = q.shape[1]
    kw = kv_heads * hd
    b, wb = cache_k.shape[:2]
    n_seq = min(n_seq, b)
    t = n_seq * dec_seq
    group = qw // kw
    return pl.pallas_call(
        functools.partial(_swa_sample_kernel, kv_heads=kv_heads, group=group, hd=hd, dec_seq=dec_seq, scale=scale),
        grid=(b // n_seq,),
        in_specs=[pl.BlockSpec(memory_space=pltpu.SMEM),
                  pl.BlockSpec((t, qw), lambda i: (i, 0)),
                  pl.BlockSpec((t, 2 * kw), lambda i: (i, 0)),
                  pl.BlockSpec((n_seq, wb, kw), lambda i: (i, 0, 0)),
                  pl.BlockSpec((n_seq, wb, kw), lambda i: (i, 0, 0)),
                  pl.BlockSpec((t, d), lambda i: (i, 0)),
                  _const_spec(w_o.shape), _const_spec((1, d))],
        out_specs=pl.BlockSpec((t, d), lambda i: (i, 0)),
        out_shape=jax.ShapeDtypeStruct((m, d), F32),
        scratch_shapes=[pltpu.VMEM((t, qw), F32)],
        compiler_params=_cparams(1),
        name="swa_sample_attn",
    )(sinks, q, kv_new, cache_k.reshape(b, wb, kw), cache_v.reshape(b, wb, kw), x, w_o, b_o.reshape(1, d))


def _prep_weights(p):
    n_a, kv_lora, heads, nope = p["mla_w_uk"].shape
    q_lora = p["mla_w_dq"].shape[2]
    d = p["mla_w_o"].shape[2]
    v_dim = p["mla_w_uv"].shape[3]
    w = {"ffn_in": p["ffn_w_in"].astype(BF16), "ffn_out": p["ffn_w_out"].astype(BF16), "mla": []}
    for l in range(n_a):
        w_uq = p["mla_w_uq"][l].reshape(q_lora, heads, nope + ROPE_DIM)
        w_dkv = p["mla_w_dkv"][l]
        w["mla"].append({
            "norm": p["mla_norm"][l], "q_norm": p["mla_q_norm"][l], "kv_norm": p["mla_kv_norm"][l],
            "w_dq": p["mla_w_dq"][l].astype(BF16),
            "w_uq_n": w_uq[:, :, :nope].reshape(q_lora, heads * nope).astype(BF16),
            "w_uq_r": w_uq[:, :, nope:].reshape(q_lora, heads * ROPE_DIM).astype(BF16),
            "w_uk": jnp.transpose(p["mla_w_uk"][l], (1, 2, 0)).astype(BF16),
            "w_dkv_c": w_dkv[:, :kv_lora].astype(BF16),
            "w_dkv_r": jnp.pad(w_dkv[:, kv_lora:], ((0, 0), (0, LANES - ROPE_DIM))).astype(BF16),
            "w_uv": jnp.transpose(p["mla_w_uv"][l], (1, 0, 2)).astype(BF16),
            "w_o": p["mla_w_o"][l].reshape(heads, v_dim, d).astype(BF16),
        })
    w["swa_kv"] = p["swa_w_kv"].astype(BF16)
    w["swa_q"] = p["swa_w_q"].astype(BF16)
    w["swa_o"] = p["swa_w_o"].astype(BF16)
    return w


def _run_trunk(x, pos, p, w, mla_q_scale, mla_attend, swa_attend):
    depth = p["ffn_norm"].shape[0]
    n_a = p["mla_norm"].shape[0]
    kv_w = w["swa_kv"].shape[1]
    cos_t, sin_t = _rope_tables(pos)
    rows_all, kv = [], None
    for l in range(depth):
        if l == n_a:
            kv = _norm_linear(x, p["kv_norm"], w["swa_kv"], p["swa_b_kv"], cos_t, sin_t, rope_cols=kv_w // 2)
        x = _ffn(x, p["ffn_norm"][l, 0], w["ffn_in"][l, 0], w["ffn_out"][l, 0])
        if l < n_a:
            wl = w["mla"][l]
            q, rows, rows_bf = _mla_proj(x, wl, cos_t, sin_t, mla_q_scale)
            o_lat = mla_attend(l, q, rows, rows_bf)
            x = _mla_out(o_lat, x, wl["w_uv"], wl["w_o"])
            rows_all.append(rows)
        else:
            jl = l - n_a
            q = _norm_linear(x, p["swa_norm"][jl], w["swa_q"][jl], p["swa_b_q"][jl], cos_t, sin_t,
                             rope_cols=w["swa_q"].shape[2])
            x = swa_attend(q, kv, x, p["swa_sinks"][jl], w["swa_o"][jl], p["swa_b_o"][jl])
        x = _ffn(x, p["ffn_norm"][l, 1], w["ffn_in"][l, 1], w["ffn_out"][l, 1],
                 final_g=p["final_norm"] if l == depth - 1 else None)
    return x, jnp.stack(rows_all, axis=0), kv


def kernel(x_prompt, x_sample, cache_mla, cache_swa_k, cache_swa_v, page_table, ffn_norm, ffn_w_in, ffn_w_out, mla_norm, mla_w_dq, mla_q_norm, mla_w_uq, mla_w_dkv, mla_kv_norm, mla_w_uk, mla_w_uv, mla_w_o, kv_norm, swa_w_kv, swa_b_kv, swa_norm, swa_w_q, swa_b_q, swa_sinks, swa_w_o, swa_b_o, final_norm):
    p = {
        "ffn_norm": ffn_norm, "ffn_w_in": ffn_w_in, "ffn_w_out": ffn_w_out,
        "mla_norm": mla_norm, "mla_w_dq": mla_w_dq, "mla_q_norm": mla_q_norm, "mla_w_uq": mla_w_uq,
        "mla_w_dkv": mla_w_dkv, "mla_kv_norm": mla_kv_norm, "mla_w_uk": mla_w_uk, "mla_w_uv": mla_w_uv,
        "mla_w_o": mla_w_o,
        "kv_norm": kv_norm, "swa_w_kv": swa_w_kv, "swa_b_kv": swa_b_kv,
        "swa_norm": swa_norm, "swa_w_q": swa_w_q, "swa_b_q": swa_b_q, "swa_sinks": swa_sinks,
        "swa_w_o": swa_w_o, "swa_b_o": swa_b_o, "final_norm": final_norm,
    }
    batch, seq, d = x_prompt.shape
    dec_batch, dec_seq, _ = x_sample.shape
    _, kv_lora, heads, nope = mla_w_uk.shape
    e = kv_lora + ROPE_DIM
    kv_heads, hd = cache_swa_k.shape[2:]
    past_len = page_table.shape[1] * cache_mla.shape[2]
    mla_q_scale = float(nope + ROPE_DIM) ** -0.5 * LOG2_E
    swa_scale = float(hd) ** -0.5
    cache_t = jnp.swapaxes(cache_mla, 2, 3)
    assert batch == 1 and seq % Q_BLOCK == 0 and cache_swa_k.shape[1] == Q_BLOCK
    w = _prep_weights(p)

    def mla_prompt(l, q, rows, rows_bf):
        return _mla_prompt_attn(q, rows_bf, kv_lora)

    def swa_prompt(q, kv, x, sinks, w_o, b_o):
        return _swa_prompt_attn(q, kv, x, sinks, w_o, b_o, kv_heads, hd, swa_scale)

    y_p, rows_p, kv_p = _run_trunk(x_prompt.reshape(seq, d), jnp.arange(seq, dtype=jnp.int32), p, w, mla_q_scale,
                                   mla_prompt, swa_prompt)

    n_tok = dec_batch * dec_seq

    def mla_sample(l, q, rows, rows_bf):
        q_tok = jnp.transpose(q, (0, 2, 1, 3)).reshape(dec_batch, dec_seq, heads, e)
        q_seq = jnp.transpose(q_tok, (0, 2, 1, 3)).reshape(dec_batch, heads * dec_seq, e)
        o = _mla_sample_attn(q_seq, rows.reshape(dec_batch, dec_seq, e), cache_t, l, page_table, kv_lora)
        o = jnp.transpose(o.reshape(dec_batch, heads, dec_seq, kv_lora), (1, 0, 2, 3))
        return o.reshape(1, heads, n_tok, kv_lora)

    def swa_sample(q, kv, x, sinks, w_o, b_o):
        return _swa_sample_attn(q, kv, cache_swa_k, cache_swa_v, x, sinks, w_o, b_o, kv_heads, hd, dec_seq, swa_scale)

    pos_s = jnp.tile(past_len + jnp.arange(dec_seq, dtype=jnp.int32), dec_batch)
    y_s, rows_s, kv_s = _run_trunk(x_sample.reshape(n_tok, d), pos_s, p, w, mla_q_scale, mla_sample, swa_sample)

    kw = kv_heads * hd
    w_p = min(Q_BLOCK, seq)
    k_p = kv_p[seq - w_p:, :kw].reshape(1, w_p, kv_heads, hd)
    v_p = kv_p[seq - w_p:, kw:].reshape(1, w_p, kv_heads, hd)
    k_s = kv_s[:, :kw].reshape(dec_batch, dec_seq, kv_heads, hd)
    v_s = kv_s[:, kw:].reshape(dec_batch, dec_seq, kv_heads, hd)
    swa_k_sample = jnp.concatenate([cache_swa_k, k_s], axis=1)[:, dec_seq:]
    swa_v_sample = jnp.concatenate([cache_swa_v, v_s], axis=1)[:, dec_seq:]
    return (y_p.reshape(1, seq, d), y_s.reshape(dec_batch, dec_seq, d),
            rows_p.reshape(-1, 1, seq, e), rows_s.reshape(-1, dec_batch, dec_seq, e),
            k_p, v_p, swa_k_sample, swa_v_sample)
```

```python
import functools

import jax
import jax.numpy as jnp
from jax import lax
from jax.experimental import pallas as pl
from jax.experimental.pallas import tpu as pltpu

F32 = jnp.float32
BF16 = jnp.bfloat16

EPS = 1e-6
ROPE_THETA = 10000.0
FFN_RES_WEIGHT = 0.5
LANES = 128
ROPE_DIM = 64
Q_BLOCK = 128
VMEM_LIMIT = 56 * 1024 * 1024
NEG_INF = float("-inf")
LOG2_E = 1.4426950408889634


def _cparams(n_axes, vmem=None):
    return pltpu.CompilerParams(dimension_semantics=("arbitrary",) * n_axes, vmem_limit_bytes=vmem)


def _const_spec(shape):
    nd = len(shape)
    return pl.BlockSpec(shape, lambda *_: (0,) * nd, pipeline_mode=pl.Buffered(1))


def _dot(a, b):
    return jnp.dot(a, b, preferred_element_type=F32)


def _dot_nt(a, b):
    return lax.dot_general(a, b, (((1,), (1,)), ((), ())), preferred_element_type=F32)


def _idiv(x, n):
    if n & (n - 1) == 0:
        return x >> (n.bit_length() - 1)
    return lax.div(x, jnp.full(x.shape, n, x.dtype))


def _irem(x, n):
    if n & (n - 1) == 0:
        return x & (n - 1)
    return lax.rem(x, jnp.full(x.shape, n, x.dtype))


def _rms(x, g):
    return x * lax.rsqrt(jnp.mean(x * x, axis=-1, keepdims=True) + EPS) * g


def _rope128(x, cos_t, sin_t):
    lane = lax.broadcasted_iota(jnp.int32, x.shape, 1)
    first_half = (lane & (ROPE_DIM // 2)) == 0
    partner = jnp.where(first_half, pltpu.roll(x, LANES - ROPE_DIM // 2, 1), pltpu.roll(x, ROPE_DIM // 2, 1))
    return x * cos_t + partner * sin_t


def _rope_tables(pos):
    half = ROPE_DIM // 2
    inv_freq = jnp.exp(-(jnp.arange(half, dtype=F32) / half) * jnp.log(F32(ROPE_THETA)))
    ang = pos.astype(F32)[:, None] * inv_freq[None, :]
    cos, sin = jnp.cos(ang), jnp.sin(ang)
    reps = LANES // ROPE_DIM
    cos_t = jnp.tile(cos, (1, 2 * reps))
    sin_t = jnp.tile(jnp.concatenate([-sin, sin], axis=1), (1, reps))
    return cos_t, sin_t


def _ffn_kernel(x_ref, g_ref, win_ref, wout_ref, *rest, d_ff, n_chunks, final):
    if final:
        gf_ref, o_ref = rest
    else:
        (o_ref,) = rest
    x = x_ref[...]
    h = _rms(x, g_ref[...]).astype(BF16)
    ck = d_ff // n_chunks
    acc = None
    for c in range(n_chunks):
        gate = _dot(h, win_ref[:, c * ck:(c + 1) * ck])
        up = _dot(h, win_ref[:, d_ff + c * ck:d_ff + (c + 1) * ck])
        act = (gate * (1.0 / (1.0 + jnp.exp(-gate))) * up).astype(BF16)
        part = _dot(act, wout_ref[c * ck:(c + 1) * ck, :])
        acc = part if acc is None else acc + part
    y = x + FFN_RES_WEIGHT * acc
    if final:
        y = _rms(y, gf_ref[...])
    o_ref[...] = y


def _ffn(x, g, w_in, w_out, final_g=None, tm=512):
    m, d = x.shape
    d_ff = w_out.shape[0]
    tm = min(tm, m)
    n_chunks = 2 if d_ff % (2 * LANES) == 0 else 1
    in_specs = [
        pl.BlockSpec((tm, d), lambda i: (i, 0)),
        _const_spec((1, d)),
        _const_spec((d, 2 * d_ff)),
        _const_spec((d_ff, d)),
    ]
    args = [x, g.reshape(1, d), w_in, w_out]
    if final_g is not None:
        in_specs.append(_const_spec((1, d)))
        args.append(final_g.reshape(1, d))
    return pl.pallas_call(
        functools.partial(_ffn_kernel, d_ff=d_ff, n_chunks=n_chunks, final=final_g is not None),
        grid=(m // tm,),
        in_specs=in_specs,
        out_specs=pl.BlockSpec((tm, d), lambda i: (i, 0)),
        out_shape=jax.ShapeDtypeStruct((m, d), F32),
        compiler_params=_cparams(1, VMEM_LIMIT),
        name="ffn_half",
    )(*args)


def _mla_proj_kernel(x_ref, g_ref, wdq_ref, gq_ref, wuqn_ref, wuqr_ref, wuk_ref, wdkvc_ref, wdkvr_ref,
                     gkv_ref, cos_ref, sin_ref, q_ref, rows_ref, rowsb_ref, *, heads, nope, kv_lora, q_scale):
    tm = x_ref.shape[0]
    cos_t, sin_t = cos_ref[...], sin_ref[...]
    h = _rms(x_ref[...], g_ref[...]).astype(BF16)
    cq = _rms(_dot(h, wdq_ref[...]), gq_ref[...]).astype(BF16)
    qn = _dot(cq, wuqn_ref[...])
    qr = _dot(cq, wuqr_ref[...])
    qr = jnp.concatenate(
        [_rope128(qr[:, c * LANES:(c + 1) * LANES], cos_t, sin_t) for c in range(qr.shape[1] // LANES)], axis=1)
    c_lat = _rms(_dot(h, wdkvc_ref[...]), gkv_ref[...])
    k_pe = _rope128(_dot(h, wdkvr_ref[...]), cos_t, sin_t)[:, :ROPE_DIM]
    rows_ref[:, :kv_lora] = c_lat
    rows_ref[:, kv_lora:] = k_pe
    rowsb_ref[:, :kv_lora] = c_lat.astype(BF16)
    rowsb_ref[:, kv_lora:] = k_pe.astype(BF16)
    for hd in range(heads):
        q_lat = (_dot(qn[:, hd * nope:(hd + 1) * nope].astype(BF16), wuk_ref[hd]) * q_scale).astype(BF16)
        q_pe = (qr[:, hd * ROPE_DIM:(hd + 1) * ROPE_DIM] * q_scale).astype(BF16)
        for r in range(tm // Q_BLOCK):
            q_ref[r, hd, :, :kv_lora] = q_lat[r * Q_BLOCK:(r + 1) * Q_BLOCK]
            q_ref[r, hd, :, kv_lora:] = q_pe[r * Q_BLOCK:(r + 1) * Q_BLOCK]


def _mla_proj(x, w, cos_t, sin_t, q_scale, tm=512):
    m, d = x.shape
    heads, nope, kv_lora = w["w_uk"].shape
    e = kv_lora + ROPE_DIM
    q_lora = w["w_dq"].shape[1]
    tm = min(tm, m)
    nb = tm // Q_BLOCK
    consts = [w["norm"].reshape(1, d), w["w_dq"], w["q_norm"].reshape(1, q_lora), w["w_uq_n"], w["w_uq_r"],
              w["w_uk"], w["w_dkv_c"], w["w_dkv_r"], w["kv_norm"].reshape(1, kv_lora)]
    row_spec = lambda width: pl.BlockSpec((tm, width), lambda i: (i, 0))
    return pl.pallas_call(
        functools.partial(_mla_proj_kernel, heads=heads, nope=nope, kv_lora=kv_lora, q_scale=q_scale),
        grid=(m // tm,),
        in_specs=[row_spec(d)] + [_const_spec(c.shape) for c in consts] + [row_spec(LANES), row_spec(LANES)],
        out_specs=[pl.BlockSpec((nb, heads, Q_BLOCK, e), lambda i: (i, 0, 0, 0)), row_spec(e), row_spec(e)],
        out_shape=[jax.ShapeDtypeStruct((m // Q_BLOCK, heads, Q_BLOCK, e), BF16),
                   jax.ShapeDtypeStruct((m, e), F32),
                   jax.ShapeDtypeStruct((m, e), BF16)],
        compiler_params=_cparams(1),
        name="mla_proj",
    )(x, *consts, cos_t, sin_t)


def _flash_scores(q_ref, k_ref, s_scr, slot, start, *, tk):
    heads, _, e = q_ref.shape[1:]
    s_scr[slot] = _dot_nt(q_ref[0].reshape(heads * Q_BLOCK, e), k_ref[pl.ds(start, tk), :])


def _flash_update(k_ref, s_scr, slot, start, q_pos0, m_ref, l_ref, acc_ref, *, tk, kv_lora):
    s = s_scr[slot]
    if q_pos0 is not None:
        q_pos = q_pos0 + (lax.broadcasted_iota(jnp.int32, s.shape, 0) & (Q_BLOCK - 1))
        k_pos = start + lax.broadcasted_iota(jnp.int32, s.shape, 1)
        s = jnp.where(k_pos <= q_pos, s, NEG_INF)
    chunks = [s[:, j * LANES:(j + 1) * LANES] for j in range(tk // LANES)]
    m_prev = m_ref[...]
    m_new = jnp.maximum(m_prev, jnp.max(functools.reduce(jnp.maximum, chunks), axis=-1, keepdims=True))
    alpha = jnp.exp2(m_prev - m_new)
    ps = [jnp.exp2(ch - m_new) for ch in chunks]
    l_ref[...] = alpha * l_ref[...] + functools.reduce(jnp.add, ps)
    pv = _dot(jnp.concatenate(ps, axis=1).astype(BF16), k_ref[pl.ds(start, tk), :kv_lora])
    acc_ref[...] = acc_ref[...] * jnp.concatenate([alpha] * (kv_lora // LANES), axis=1) + pv
    m_ref[...] = m_new


def _mla_prompt_kernel(q_ref, k_ref, o_ref, s_scr, m_ref, l_ref, acc_ref, *, tk, kv_lora):
    i = pl.program_id(0)
    heads = q_ref.shape[1]
    m_ref[...] = jnp.full(m_ref.shape, NEG_INF, F32)
    l_ref[...] = jnp.zeros(l_ref.shape, F32)
    acc_ref[...] = jnp.zeros(acc_ref.shape, F32)
    scores = functools.partial(_flash_scores, q_ref, k_ref, s_scr, tk=tk)
    update = functools.partial(_flash_update, k_ref, s_scr, m_ref=m_ref, l_ref=l_ref, acc_ref=acc_ref,
                               tk=tk, kv_lora=kv_lora)
    at = lambda j: pl.multiple_of(j * tk, tk)
    n_full = (i * Q_BLOCK) // tk
    n_pairs = n_full // 2
    scores(0, at(0))

    def body(p, carry):
        j = 2 * p
        scores(1, at(j + 1))
        update(0, at(j), None)
        scores(0, at(j + 2))
        update(1, at(j + 1), None)
        return carry

    lax.fori_loop(0, n_pairs, body, 0)
    j0 = 2 * n_pairs
    q_pos0 = i * Q_BLOCK

    @pl.when(n_full == j0)
    def _():
        update(0, at(j0), q_pos0)

    @pl.when(n_full != j0)
    def _():
        scores(1, at(j0 + 1))
        update(0, at(j0), None)
        update(1, at(j0 + 1), q_pos0)

    l_tot = jnp.sum(l_ref[...], axis=-1, keepdims=True)
    o_ref[0] = (acc_ref[...] / l_tot).astype(o_ref.dtype).reshape(heads, Q_BLOCK, kv_lora)


def _mla_prompt_attn(q, rows_bf, kv_lora, tk=1024):
    nb, heads, _, e = q.shape
    s = rows_bf.shape[0]
    tk = min(tk, s)
    assert s % tk == 0 and tk % Q_BLOCK == 0
    rows = heads * Q_BLOCK
    return pl.pallas_call(
        functools.partial(_mla_prompt_kernel, tk=tk, kv_lora=kv_lora),
        grid=(nb,),
        in_specs=[pl.BlockSpec((1, heads, Q_BLOCK, e), lambda i: (i, 0, 0, 0)), _const_spec((s, e))],
        out_specs=pl.BlockSpec((1, heads, Q_BLOCK, kv_lora), lambda i: (i, 0, 0, 0)),
        out_shape=jax.ShapeDtypeStruct((nb, heads, Q_BLOCK, kv_lora), BF16),
        scratch_shapes=[pltpu.VMEM((2, rows, tk), F32), pltpu.VMEM((rows, LANES), F32),
                        pltpu.VMEM((rows, LANES), F32), pltpu.VMEM((rows, kv_lora), F32)],
        compiler_params=_cparams(1, VMEM_LIMIT),
        name="mla_prompt_attn",
    )(q, rows_bf)


def _mla_sample_kernel(pt_ref, q_ref, new_ref, cache_ref, o_ref, pbuf, sem, kbuf_ref, nbuf_ref, m_ref, l_ref, acc_ref,
                       *, layer, n_group, dec_seq, kv_lora):
    p_idx = pl.program_id(1)
    n_steps = pl.num_programs(0) * pl.num_programs(1)
    t = pl.program_id(0) * pl.num_programs(1) + p_idx
    n_slots = pbuf.shape[0]
    ahead = n_slots - 1
    slot = lax.rem(t, n_slots)
    page = pbuf.shape[3]

    def page_copies(step, sl):
        return [pltpu.make_async_copy(cache_ref.at[layer, pt_ref[step * n_group + g]], pbuf.at[sl, g], sem.at[sl])
                for g in range(n_group)]

    @pl.when(t == 0)
    def _():
        for step in range(ahead):
            @pl.when(step < n_steps)
            def _():
                for c in page_copies(step, step):
                    c.start()

    @pl.when(t + ahead < n_steps)
    def _():
        for c in page_copies(t + ahead, lax.rem(t + ahead, n_slots)):
            c.start()

    q = q_ref[...]

    @pl.when(p_idx == 0)
    def _():
        nbuf_ref[...] = jnp.zeros(nbuf_ref.shape, F32)
        nbuf_ref[:dec_seq, :] = new_ref[...]
        nk = nbuf_ref[...].astype(BF16)
        s = _dot_nt(q, nk)
        tok = _irem(lax.broadcasted_iota(jnp.int32, s.shape, 0), dec_seq)
        s = jnp.where(lax.broadcasted_iota(jnp.int32, s.shape, 1) <= tok, s, NEG_INF)
        m0 = jnp.max(s, axis=-1, keepdims=True)
        p0 = jnp.exp2(s - m0)
        m_ref[...] = m0
        l_ref[...] = jnp.sum(p0, axis=-1, keepdims=True)
        acc_ref[...] = _dot(p0.astype(BF16), nk[:, :kv_lora])

    for c in page_copies(t, slot):
        c.wait()
    for g in range(n_group):
        kbuf_ref[:, g * page:(g + 1) * page] = pbuf[slot, g].astype(BF16)
    kt = kbuf_ref[...]
    s = _dot(q, kt)
    m_prev = m_ref[...]
    m_new = jnp.maximum(m_prev, jnp.max(s, axis=-1, keepdims=True))
    alpha = jnp.exp2(m_prev - m_new)
    p = jnp.exp2(s - m_new)
    l_ref[...] = alpha * l_ref[...] + jnp.sum(p, axis=-1, keepdims=True)
    acc_ref[...] = alpha * acc_ref[...] + _dot_nt(p.astype(BF16), kt[:kv_lora, :])
    m_ref[...] = m_new

    @pl.when(p_idx == pl.num_programs(1) - 1)
    def _():
        o_ref[...] = (acc_ref[...] / l_ref[...]).astype(o_ref.dtype)


def _mla_sample_attn(q, new_rows, cache_t, layer, page_table, kv_lora, n_group=32, n_slots=3):
    b, qrows, e = q.shape
    dec_seq = new_rows.shape[1]
    n_pages = page_table.shape[1]
    page = cache_t.shape[3]
    n_group = min(n_group, n_pages)
    assert n_pages % n_group == 0 and dec_seq <= page

    grid_spec = pltpu.PrefetchScalarGridSpec(
        num_scalar_prefetch=1,
        grid=(b, n_pages // n_group),
        in_specs=[pl.BlockSpec((None, qrows, e), lambda bi, pi, pt: (bi, 0, 0)),
                  pl.BlockSpec((None, dec_seq, e), lambda bi, pi, pt: (bi, 0, 0)),
                  pl.BlockSpec(memory_space=pl.ANY)],
        out_specs=pl.BlockSpec((None, qrows, kv_lora), lambda bi, pi, pt: (bi, 0, 0)),
        scratch_shapes=[pltpu.VMEM((n_slots, n_group, e, page), F32), pltpu.SemaphoreType.DMA((n_slots,)),
                        pltpu.VMEM((e, n_group * page), BF16), pltpu.VMEM((page, e), F32),
                        pltpu.VMEM((qrows, 1), F32), pltpu.VMEM((qrows, 1), F32), pltpu.VMEM((qrows, kv_lora), F32)],
    )
    return pl.pallas_call(
        functools.partial(_mla_sample_kernel, layer=layer, n_group=n_group, dec_seq=dec_seq, kv_lora=kv_lora),
        grid_spec=grid_spec,
        out_shape=jax.ShapeDtypeStruct((b, qrows, kv_lora), BF16),
        compiler_params=_cparams(2),
        name="mla_sample_attn",
    )(page_table.reshape(-1), q, new_rows, cache_t)


def _mla_out_kernel(o_ref, x_ref, wuv_ref, wo_ref, y_ref):
    nblk, heads, tq, kv_lora = o_ref.shape
    acc = x_ref[...]
    for hd in range(heads):
        o_h = _dot(o_ref[:, hd].reshape(nblk * tq, kv_lora), wuv_ref[hd]).astype(BF16)
        acc = acc + _dot(o_h, wo_ref[hd])
    y_ref[...] = acc


def _mla_out(o_lat, x, w_uv, w_o, rows_per_step=512):
    nb, heads, tq, kv_lora = o_lat.shape
    d = x.shape[1]
    assert o_lat.dtype == BF16
    nblk = max(1, min(nb, rows_per_step // tq))
    assert nb % nblk == 0
    return pl.pallas_call(
        _mla_out_kernel,
        grid=(nb // nblk,),
        in_specs=[pl.BlockSpec((nblk, heads, tq, kv_lora), lambda i: (i, 0, 0, 0)),
                  pl.BlockSpec((nblk * tq, d), lambda i: (i, 0)),
                  _const_spec(w_uv.shape), _const_spec(w_o.shape)],
        out_specs=pl.BlockSpec((nblk * tq, d), lambda i: (i, 0)),
        out_shape=jax.ShapeDtypeStruct(x.shape, F32),
        compiler_params=_cparams(1),
        name="mla_out",
    )(o_lat, x, w_uv, w_o)


def _norm_linear_kernel(x_ref, g_ref, w_ref, b_ref, cos_ref, sin_ref, y_ref, *, rope_cols):
    h = _rms(x_ref[...], g_ref[...]).astype(BF16)
    y = _dot(h, w_ref[...]) + b_ref[...]
    cos_t, sin_t = cos_ref[...], sin_ref[...]
    for c in range(rope_cols // LANES):
        y_ref[:, c * LANES:(c + 1) * LANES] = _rope128(y[:, c * LANES:(c + 1) * LANES], cos_t, sin_t)
    if rope_cols < y.shape[1]:
        y_ref[:, rope_cols:] = y[:, rope_cols:]


def _norm_linear(x, g, w, b, cos_t, sin_t, rope_cols, tm=512):
    m, d = x.shape
    n = w.shape[1]
    tm = min(tm, m)
    row_spec = lambda width: pl.BlockSpec((tm, width), lambda i: (i, 0))
    return pl.pallas_call(
        functools.partial(_norm_linear_kernel, rope_cols=rope_cols),
        grid=(m // tm,),
        in_specs=[row_spec(d), _const_spec((1, d)), _const_spec((d, n)), _const_spec((1, n)),
                  row_spec(LANES), row_spec(LANES)],
        out_specs=row_spec(n),
        out_shape=jax.ShapeDtypeStruct((m, n), F32),
        compiler_params=_cparams(1),
        name="norm_linear_rope",
    )(x, g.reshape(1, d), w, b.reshape(1, n), cos_t, sin_t)


def _sink_attend(s, valid, sink, v):
    s = jnp.where(valid, s, NEG_INF)
    m = jnp.maximum(jnp.max(s, axis=-1, keepdims=True), sink)
    p = jnp.exp(s - m)
    pr = p / (jnp.sum(p, axis=-1, keepdims=True) + jnp.exp(sink - m))
    return _dot(pr.astype(BF16), v)


def _stack_group(q, kvh, group, hd):
    return jnp.concatenate([q[:, (kvh * group + g) * hd:(kvh * group + g + 1) * hd] for g in range(group)], axis=0)


def _sink_rows(sinks_ref, kvh, group, t):
    row = lax.broadcasted_iota(jnp.int32, (group * t, 1), 0)
    sink = jnp.full((group * t, 1), sinks_ref[kvh * group], F32)
    for g in range(1, group):
        sink = jnp.where(row >= g * t, sinks_ref[kvh * group + g], sink)
    return sink


def _swa_finish(o_scr, x_ref, wo_ref, bo_ref, y_ref):
    y_ref[...] = x_ref[...] + _dot(o_scr[...].astype(BF16), wo_ref[...]) + bo_ref[...]


def _swa_prompt_kernel(sinks_ref, q_ref, cur_ref, pk_ref, pv_ref, x_ref, wo_ref, bo_ref, y_ref, o_scr,
                       *, kv_heads, group, hd, scale):
    i = pl.program_id(0)
    t = q_ref.shape[0]
    kw = kv_heads * hd
    q = q_ref[...]
    kk = jnp.concatenate([pk_ref[...], cur_ref[:, :kw]], axis=0).astype(BF16)
    vv = jnp.concatenate([pv_ref[...], cur_ref[:, kw:]], axis=0).astype(BF16)
    shape = (group * t, 2 * t)
    r = _irem(lax.broadcasted_iota(jnp.int32, shape, 0), t)
    j = lax.broadcasted_iota(jnp.int32, shape, 1)
    valid = (j > r) & (j <= r + t) & ((j >= t) | (i > 0))
    for kvh in range(kv_heads):
        qg = _stack_group(q, kvh, group, hd).astype(BF16)
        s = _dot_nt(qg, kk[:, kvh * hd:(kvh + 1) * hd]) * scale
        o = _sink_attend(s, valid, _sink_rows(sinks_ref, kvh, group, t), vv[:, kvh * hd:(kvh + 1) * hd])
        for g in range(group):
            hh = kvh * group + g
            o_scr[:, hh * hd:(hh + 1) * hd] = o[g * t:(g + 1) * t]
    _swa_finish(o_scr, x_ref, wo_ref, bo_ref, y_ref)


def _swa_prompt_attn(q, kv, x, sinks, w_o, b_o, kv_heads, hd, scale):
    m, d = x.shape
    qw = q.shape[1]
    kw = kv_heads * hd
    t = Q_BLOCK
    group = qw // kw
    prev = lambda col: pl.BlockSpec((t, kw), lambda i: (jnp.maximum(i - 1, 0), col))
    return pl.pallas_call(
        functools.partial(_swa_prompt_kernel, kv_heads=kv_heads, group=group, hd=hd, scale=scale),
        grid=(m // t,),
        in_specs=[pl.BlockSpec(memory_space=pltpu.SMEM),
                  pl.BlockSpec((t, qw), lambda i: (i, 0)),
                  pl.BlockSpec((t, 2 * kw), lambda i: (i, 0)),
                  prev(0), prev(1),
                  pl.BlockSpec((t, d), lambda i: (i, 0)),
                  _const_spec(w_o.shape), _const_spec((1, d))],
        out_specs=pl.BlockSpec((t, d), lambda i: (i, 0)),
        out_shape=jax.ShapeDtypeStruct((m, d), F32),
        scratch_shapes=[pltpu.VMEM((t, qw), F32)],
        compiler_params=_cparams(1),
        name="swa_prompt_attn",
    )(sinks, q, kv, kv, kv, x, w_o, b_o.reshape(1, d))


def _swa_sample_kernel(sinks_ref, q_ref, new_ref, ck_ref, cv_ref, x_ref, wo_ref, bo_ref, y_ref, o_scr,
                       *, kv_heads, group, hd, dec_seq, scale):
    n_seq, wb, kw = ck_ref.shape
    t = n_seq * dec_seq
    q = q_ref[...]
    kk = jnp.concatenate([ck_ref[...].reshape(n_seq * wb, kw), new_ref[:, :kw]], axis=0).astype(BF16)
    vv = jnp.concatenate([cv_ref[...].reshape(n_seq * wb, kw), new_ref[:, kw:]], axis=0).astype(BF16)
    n_old = n_seq * wb
    shape = (group * t, n_old + t)
    r = _irem(lax.broadcasted_iota(jnp.int32, shape, 0), t)
    r_seq, r_tok = _idiv(r, dec_seq), _irem(r, dec_seq)
    j = lax.broadcasted_iota(jnp.int32, shape, 1)
    old = j < n_old
    j_new = jnp.maximum(j - n_old, 0)
    j_seq = jnp.where(old, _idiv(j, wb), _idiv(j_new, dec_seq))
    j_pos = jnp.where(old, _irem(j, wb), wb + _irem(j_new, dec_seq))
    valid = (j_seq == r_seq) & (j_pos > r_tok) & (j_pos <= r_tok + wb)
    for kvh in range(kv_heads):
        qg = _stack_group(q, kvh, group, hd).astype(BF16)
        s = _dot_nt(qg, kk[:, kvh * hd:(kvh + 1) * hd]) * scale
        o = _sink_attend(s, valid, _sink_rows(sinks_ref, kvh, group, t), vv[:, kvh * hd:(kvh + 1) * hd])
        for g in range(group):
            hh = kvh * group + g
            o_scr[:, hh * hd:(hh + 1) * hd] = o[g * t:(g + 1) * t]
    _swa_finish(o_scr, x_ref, wo_ref, bo_ref, y_ref)


def _swa_sample_attn(q, kv_new, cache_k, cache_v, x, sinks, w_o, b_o, kv_heads, hd, dec_seq, scale, n_seq=8):
    m, d = x.shape
    qw = q.shape[1]
    kw = kv_heads * hd
    b, wb = cache_k.shape[:2]
    n_seq = min(n_seq, b)
    t = n_seq * dec_seq
    group = qw // kw
    return pl.pallas_call(
        functools.partial(_swa_sample_kernel, kv_heads=kv_heads, group=group, hd=hd, dec_seq=dec_seq, scale=scale),
        grid=(b // n_seq,),
        in_specs=[pl.BlockSpec(memory_space=pltpu.SMEM),
                  pl.BlockSpec((t, qw), lambda i: (i, 0)),
                  pl.BlockSpec((t, 2 * kw), lambda i: (i, 0)),
                  pl.BlockSpec((n_seq, wb, kw), lambda i: (i, 0, 0)),
                  pl.BlockSpec((n_seq, wb, kw), lambda i: (i, 0, 0)),
                  pl.BlockSpec((t, d), lambda i: (i, 0)),
                  _const_spec(w_o.shape), _const_spec((1, d))],
        out_specs=pl.BlockSpec((t, d), lambda i: (i, 0)),
        out_shape=jax.ShapeDtypeStruct((m, d), F32),
        scratch_shapes=[pltpu.VMEM((t, qw), F32)],
        compiler_params=_cparams(1),
        name="swa_sample_attn",
    )(sinks, q, kv_new, cache_k.reshape(b, wb, kw), cache_v.reshape(b, wb, kw), x, w_o, b_o.reshape(1, d))


def _prep_weights(p):
    n_a, kv_lora, heads, nope = p["mla_w_uk"].shape
    q_lora = p["mla_w_dq"].shape[2]
    d = p["mla_w_o"].shape[2]
    v_dim = p["mla_w_uv"].shape[3]
    w = {"ffn_in": p["ffn_w_in"].astype(BF16), "ffn_out": p["ffn_w_out"].astype(BF16), "mla": []}
    for l in range(n_a):
        w_uq = p["mla_w_uq"][l].reshape(q_lora, heads, nope + ROPE_DIM)
        w_dkv = p["mla_w_dkv"][l]
        w["mla"].append({
            "norm": p["mla_norm"][l], "q_norm": p["mla_q_norm"][l], "kv_norm": p["mla_kv_norm"][l],
            "w_dq": p["mla_w_dq"][l].astype(BF16),
            "w_uq_n": w_uq[:, :, :nope].reshape(q_lora, heads * nope).astype(BF16),
            "w_uq_r": w_uq[:, :, nope:].reshape(q_lora, heads * ROPE_DIM).astype(BF16),
            "w_uk": jnp.transpose(p["mla_w_uk"][l], (1, 2, 0)).astype(BF16),
            "w_dkv_c": w_dkv[:, :kv_lora].astype(BF16),
            "w_dkv_r": jnp.pad(w_dkv[:, kv_lora:], ((0, 0), (0, LANES - ROPE_DIM))).astype(BF16),
            "w_uv": jnp.transpose(p["mla_w_uv"][l], (1, 0, 2)).astype(BF16),
            "w_o": p["mla_w_o"][l].reshape(heads, v_dim, d).astype(BF16),
        })
    w["swa_kv"] = p["swa_w_kv"].astype(BF16)
    w["swa_q"] = p["swa_w_q"].astype(BF16)
    w["swa_o"] = p["swa_w_o"].astype(BF16)
    return w


def _run_trunk(x, pos, p, w, mla_q_scale, mla_attend, swa_attend):
    depth = p["ffn_norm"].shape[0]
    n_a = p["mla_norm"].shape[0]
    kv_w = w["swa_kv"].shape[1]
    cos_t, sin_t = _rope_tables(pos)
    rows_all, kv = [], None
    for l in range(depth):
        if l == n_a:
            kv = _norm_linear(x, p["kv_norm"], w["swa_kv"], p["swa_b_kv"], cos_t, sin_t, rope_cols=kv_w // 2)
        x = _ffn(x, p["ffn_norm"][l, 0], w["ffn_in"][l, 0], w["ffn_out"][l, 0])
        if l < n_a:
            wl = w["mla"][l]
            q, rows, rows_bf = _mla_proj(x, wl, cos_t, sin_t, mla_q_scale)
            o_lat = mla_attend(l, q, rows, rows_bf)
            x = _mla_out(o_lat, x, wl["w_uv"], wl["w_o"])
            rows_all.append(rows)
        else:
            jl = l - n_a
            q = _norm_linear(x, p["swa_norm"][jl], w["swa_q"][jl], p["swa_b_q"][jl], cos_t, sin_t,
                             rope_cols=w["swa_q"].shape[2])
            x = swa_attend(q, kv, x, p["swa_sinks"][jl], w["swa_o"][jl], p["swa_b_o"][jl])
        x = _ffn(x, p["ffn_norm"][l, 1], w["ffn_in"][l, 1], w["ffn_out"][l, 1],
                 final_g=p["final_norm"] if l == depth - 1 else None)
    return x, jnp.stack(rows_all, axis=0), kv


def kernel(x_prompt, x_sample, cache_mla, cache_swa_k, cache_swa_v, page_table, ffn_norm, ffn_w_in, ffn_w_out, mla_norm, mla_w_dq, mla_q_norm, mla_w_uq, mla_w_dkv, mla_kv_norm, mla_w_uk, mla_w_uv, mla_w_o, kv_norm, swa_w_kv, swa_b_kv, swa_norm, swa_w_q, swa_b_q, swa_sinks, swa_w_o, swa_b_o, final_norm):
    p = {
        "ffn_norm": ffn_norm, "ffn_w_in": ffn_w_in, "ffn_w_out": ffn_w_out,
        "mla_norm": mla_norm, "mla_w_dq": mla_w_dq, "mla_q_norm": mla_q_norm, "mla_w_uq": mla_w_uq,
        "mla_w_dkv": mla_w_dkv, "mla_kv_norm": mla_kv_norm, "mla_w_uk": mla_w_uk, "mla_w_uv": mla_w_uv,
        "mla_w_o": mla_w_o,
        "kv_norm": kv_norm, "swa_w_kv": swa_w_kv, "swa_b_kv": swa_b_kv,
        "swa_norm": swa_norm, "swa_w_q": swa_w_q, "swa_b_q": swa_b_q, "swa_sinks": swa_sinks,
        "swa_w_o": swa_w_o, "swa_b_o": swa_b_o, "final_norm": final_norm,
    }
    batch, seq, d = x_prompt.shape
    dec_batch, dec_seq, _ = x_sample.shape
    _, kv_lora, heads, nope = mla_w_uk.shape
    e = kv_lora + ROPE_DIM
    kv_heads, hd = cache_swa_k.shape[2:]
    past_len = page_table.shape[1] * cache_mla.shape[2]
    mla_q_scale = float(nope + ROPE_DIM) ** -0.5 * LOG2_E
    swa_scale = float(hd) ** -0.5
    cache_t = jnp.swapaxes(cache_mla, 2, 3)
    assert batch == 1 and seq % Q_BLOCK == 0 and cache_swa_k.shape[1] == Q_BLOCK
    w = _prep_weights(p)

    def mla_prompt(l, q, rows, rows_bf):
        return _mla_prompt_attn(q, rows_bf, kv_lora)

    def swa_prompt(q, kv, x, sinks, w_o, b_o):
        return _swa_prompt_attn(q, kv, x, sinks, w_o, b_o, kv_heads, hd, swa_scale)

    y_p, rows_p, kv_p = _run_trunk(x_prompt.reshape(seq, d), jnp.arange(seq, dtype=jnp.int32), p, w, mla_q_scale,
                                   mla_prompt, swa_prompt)

    n_tok = dec_batch * dec_seq

    def mla_sample(l, q, rows, rows_bf):
        q_tok = jnp.transpose(q, (0, 2, 1, 3)).reshape(dec_batch, dec_seq, heads, e)
        q_seq = jnp.transpose(q_tok, (0, 2, 1, 3)).reshape(dec_batch, heads * dec_seq, e)
        o = _mla_sample_attn(q_seq, rows.reshape(dec_batch, dec_seq, e), cache_t, l, page_table, kv_lora)
        o = jnp.transpose(o.reshape(dec_batch, heads, dec_seq, kv_lora), (1, 0, 2, 3))
        return o.reshape(1, heads, n_tok, kv_lora)

    def swa_sample(q, kv, x, sinks, w_o, b_o):
        return _swa_sample_attn(q, kv, cache_swa_k, cache_swa_v, x, sinks, w_o, b_o, kv_heads, hd, dec_seq, swa_scale)

    pos_s = jnp.tile(past_len + jnp.arange(dec_seq, dtype=jnp.int32), dec_batch)
    y_s, rows_s, kv_s = _run_trunk(x_sample.reshape(n_tok, d), pos_s, p, w, mla_q_scale, mla_sample, swa_sample)

    kw = kv_heads * hd
    w_p = min(Q_BLOCK, seq)
    k_p = kv_p[seq - w_p:, :kw].reshape(1, w_p, kv_heads, hd)
    v_p = kv_p[seq - w_p:, kw:].reshape(1, w_p, kv_heads, hd)
    k_s = kv_s[:, :kw].reshape(dec_batch, dec_seq, kv_heads, hd)
    v_s = kv_s[:, kw:].reshape(dec_batch, dec_seq, kv_heads, hd)
    swa_k_sample = jnp.concatenate([cache_swa_k, k_s], axis=1)[:, dec_seq:]
    swa_v_sample = jnp.concatenate([cache_swa_v, v_s], axis=1)[:, dec_seq:]
    return (y_p.reshape(1, seq, d), y_s.reshape(dec_batch, dec_seq, d),
            rows_p.reshape(-1, 1, seq, e), rows_s.reshape(-1, dec_batch, dec_seq, e),
            k_p, v_p, swa_k_sample, swa_v_sample)
```

```python
import functools

import jax
import jax.numpy as jnp
from jax import lax
from jax.experimental import pallas as pl
from jax.experimental.pallas import tpu as pltpu

F32 = jnp.float32
BF16 = jnp.bfloat16

EPS = 1e-6
ROPE_THETA = 10000.0
FFN_RES_WEIGHT = 0.5
LANES = 128
ROPE_DIM = 64
Q_BLOCK = 128
VMEM_LIMIT = 56 * 1024 * 1024
NEG_INF = float("-inf")
LOG2_E = 1.4426950408889634


def _cparams(n_axes, vmem=None):
    return pltpu.CompilerParams(dimension_semantics=("arbitrary",) * n_axes, vmem_limit_bytes=vmem)


def _const_spec(shape):
    nd = len(shape)
    return pl.BlockSpec(shape, lambda *_: (0,) * nd, pipeline_mode=pl.Buffered(1))


def _dot(a, b):
    return jnp.dot(a, b, preferred_element_type=F32)


def _dot_nt(a, b):
    return lax.dot_general(a, b, (((1,), (1,)), ((), ())), preferred_element_type=F32)


def _idiv(x, n):
    if n & (n - 1) == 0:
        return x >> (n.bit_length() - 1)
    return lax.div(x, jnp.full(x.shape, n, x.dtype))


def _irem(x, n):
    if n & (n - 1) == 0:
        return x & (n - 1)
    return lax.rem(x, jnp.full(x.shape, n, x.dtype))


def _rms(x, g):
    return x * lax.rsqrt(jnp.mean(x * x, axis=-1, keepdims=True) + EPS) * g


def _rope128(x, cos_t, sin_t):
    lane = lax.broadcasted_iota(jnp.int32, x.shape, 1)
    first_half = (lane & (ROPE_DIM // 2)) == 0
    partner = jnp.where(first_half, pltpu.roll(x, LANES - ROPE_DIM // 2, 1), pltpu.roll(x, ROPE_DIM // 2, 1))
    return x * cos_t + partner * sin_t


def _rope_tables(pos):
    half = ROPE_DIM // 2
    inv_freq = jnp.exp(-(jnp.arange(half, dtype=F32) / half) * jnp.log(F32(ROPE_THETA)))
    ang = pos.astype(F32)[:, None] * inv_freq[None, :]
    cos, sin = jnp.cos(ang), jnp.sin(ang)
    reps = LANES // ROPE_DIM
    cos_t = jnp.tile(cos, (1, 2 * reps))
    sin_t = jnp.tile(jnp.concatenate([-sin, sin], axis=1), (1, reps))
    return cos_t, sin_t


def _ffn_kernel(x_ref, g_ref, win_ref, wout_ref, *rest, d_ff, n_chunks, final):
    if final:
        gf_ref, o_ref = rest
    else:
        (o_ref,) = rest
    x = x_ref[...]
    h = _rms(x, g_ref[...]).astype(BF16)
    ck = d_ff // n_chunks
    acc = None
    for c in range(n_chunks):
        gate = _dot(h, win_ref[:, c * ck:(c + 1) * ck])
        up = _dot(h, win_ref[:, d_ff + c * ck:d_ff + (c + 1) * ck])
        act = (gate * (1.0 / (1.0 + jnp.exp(-gate))) * up).astype(BF16)
        part = _dot(act, wout_ref[c * ck:(c + 1) * ck, :])
        acc = part if acc is None else acc + part
    y = x + FFN_RES_WEIGHT * acc
    if final:
        y = _rms(y, gf_ref[...])
    o_ref[...] = y


def _ffn(x, g, w_in, w_out, final_g=None, tm=512):
    m, d = x.shape
    d_ff = w_out.shape[0]
    tm = min(tm, m)
    n_chunks = 2 if d_ff % (2 * LANES) == 0 else 1
    in_specs = [
        pl.BlockSpec((tm, d), lambda i: (i, 0)),
        _const_spec((1, d)),
        _const_spec((d, 2 * d_ff)),
        _const_spec((d_ff, d)),
    ]
    args = [x, g.reshape(1, d), w_in, w_out]
    if final_g is not None:
        in_specs.append(_const_spec((1, d)))
        args.append(final_g.reshape(1, d))
    return pl.pallas_call(
        functools.partial(_ffn_kernel, d_ff=d_ff, n_chunks=n_chunks, final=final_g is not None),
        grid=(m // tm,),
        in_specs=in_specs,
        out_specs=pl.BlockSpec((tm, d), lambda i: (i, 0)),
        out_shape=jax.ShapeDtypeStruct((m, d), F32),
        compiler_params=_cparams(1, VMEM_LIMIT),
        name="ffn_half",
    )(*args)


def _mla_proj_kernel(x_ref, g_ref, wdq_ref, gq_ref, wuqn_ref, wuqr_ref, wuk_ref, wdkvc_ref, wdkvr_ref,
                     gkv_ref, cos_ref, sin_ref, q_ref, rows_ref, rowsb_ref, *, heads, nope, kv_lora, q_scale):
    tm = x_ref.shape[0]
    cos_t, sin_t = cos_ref[...], sin_ref[...]
    h = _rms(x_ref[...], g_ref[...]).astype(BF16)
    cq = _rms(_dot(h, wdq_ref[...]), gq_ref[...]).astype(BF16)
    qn = _dot(cq, wuqn_ref[...])
    qr = _dot(cq, wuqr_ref[...])
    qr = jnp.concatenate(
        [_rope128(qr[:, c * LANES:(c + 1) * LANES], cos_t, sin_t) for c in range(qr.shape[1] // LANES)], axis=1)
    c_lat = _rms(_dot(h, wdkvc_ref[...]), gkv_ref[...])
    k_pe = _rope128(_dot(h, wdkvr_ref[...]), cos_t, sin_t)[:, :ROPE_DIM]
    rows_ref[:, :kv_lora] = c_lat
    rows_ref[:, kv_lora:] = k_pe
    rowsb_ref[:, :kv_lora] = c_lat.astype(BF16)
    rowsb_ref[:, kv_lora:] = k_pe.astype(BF16)
    for hd in range(heads):
        q_lat = (_dot(qn[:, hd * nope:(hd + 1) * nope].astype(BF16), wuk_ref[hd]) * q_scale).astype(BF16)
        q_pe = (qr[:, hd * ROPE_DIM:(hd + 1) * ROPE_DIM] * q_scale).astype(BF16)
        for r in range(tm // Q_BLOCK):
            q_ref[r, hd, :, :kv_lora] = q_lat[r * Q_BLOCK:(r + 1) * Q_BLOCK]
            q_ref[r, hd, :, kv_lora:] = q_pe[r * Q_BLOCK:(r + 1) * Q_BLOCK]


def _mla_proj(x, w, cos_t, sin_t, q_scale, tm=512):
    m, d = x.shape
    heads, nope, kv_lora = w["w_uk"].shape
    e = kv_lora + ROPE_DIM
    q_lora = w["w_dq"].shape[1]
    tm = min(tm, m)
    nb = tm // Q_BLOCK
    consts = [w["norm"].reshape(1, d), w["w_dq"], w["q_norm"].reshape(1, q_lora), w["w_uq_n"], w["w_uq_r"],
              w["w_uk"], w["w_dkv_c"], w["w_dkv_r"], w["kv_norm"].reshape(1, kv_lora)]
    row_spec = lambda width: pl.BlockSpec((tm, width), lambda i: (i, 0))
    return pl.pallas_call(
        functools.partial(_mla_proj_kernel, heads=heads, nope=nope, kv_lora=kv_lora, q_scale=q_scale),
        grid=(m // tm,),
        in_specs=[row_spec(d)] + [_const_spec(c.shape) for c in consts] + [row_spec(LANES), row_spec(LANES)],
        out_specs=[pl.BlockSpec((nb, heads, Q_BLOCK, e), lambda i: (i, 0, 0, 0)), row_spec(e), row_spec(e)],
        out_shape=[jax.ShapeDtypeStruct((m // Q_BLOCK, heads, Q_BLOCK, e), BF16),
                   jax.ShapeDtypeStruct((m, e), F32),
                   jax.ShapeDtypeStruct((m, e), BF16)],
        compiler_params=_cparams(1),
        name="mla_proj",
    )(x, *consts, cos_t, sin_t)


def _flash_scores(q_ref, k_ref, s_scr, slot, start, *, tk):
    heads, _, e = q_ref.shape[1:]
    s_scr[slot] = _dot_nt(q_ref[0].reshape(heads * Q_BLOCK, e), k_ref[pl.ds(start, tk), :])


def _flash_update(k_ref, s_scr, slot, start, q_pos0, m_ref, l_ref, acc_ref, *, tk, kv_lora):
    s = s_scr[slot]
    if q_pos0 is not None:
        q_pos = q_pos0 + (lax.broadcasted_iota(jnp.int32, s.shape, 0) & (Q_BLOCK - 1))
        k_pos = start + lax.broadcasted_iota(jnp.int32, s.shape, 1)
        s = jnp.where(k_pos <= q_pos, s, NEG_INF)
    chunks = [s[:, j * LANES:(j + 1) * LANES] for j in range(tk // LANES)]
    m_prev = m_ref[...]
    m_new = jnp.maximum(m_prev, jnp.max(functools.reduce(jnp.maximum, chunks), axis=-1, keepdims=True))
    alpha = jnp.exp2(m_prev - m_new)
    ps = [jnp.exp2(ch - m_new) for ch in chunks]
    l_ref[...] = alpha * l_ref[...] + functools.reduce(jnp.add, ps)
    pv = _dot(jnp.concatenate(ps, axis=1).astype(BF16), k_ref[pl.ds(start, tk), :kv_lora])
    acc_ref[...] = acc_ref[...] * jnp.concatenate([alpha] * (kv_lora // LANES), axis=1) + pv
    m_ref[...] = m_new


def _mla_prompt_kernel(q_ref, k_ref, o_ref, s_scr, m_ref, l_ref, acc_ref, *, tk, kv_lora):
    i = pl.program_id(0)
    heads = q_ref.shape[1]
    m_ref[...] = jnp.full(m_ref.shape, NEG_INF, F32)
    l_ref[...] = jnp.zeros(l_ref.shape, F32)
    acc_ref[...] = jnp.zeros(acc_ref.shape, F32)
    scores = functools.partial(_flash_scores, q_ref, k_ref, s_scr, tk=tk)
    update = functools.partial(_flash_update, k_ref, s_scr, m_ref=m_ref, l_ref=l_ref, acc_ref=acc_ref,
                               tk=tk, kv_lora=kv_lora)
    at = lambda j: pl.multiple_of(j * tk, tk)
    n_full = (i * Q_BLOCK) // tk
    n_pairs = n_full // 2
    scores(0, at(0))

    def body(p, carry):
        j = 2 * p
        scores(1, at(j + 1))
        update(0, at(j), None)
        scores(0, at(j + 2))
        update(1, at(j + 1), None)
        return carry

    lax.fori_loop(0, n_pairs, body, 0)
    j0 = 2 * n_pairs
    q_pos0 = i * Q_BLOCK

    @pl.when(n_full == j0)
    def _():
        update(0, at(j0), q_pos0)

    @pl.when(n_full != j0)
    def _():
        scores(1, at(j0 + 1))
        update(0, at(j0), None)
        update(1, at(j0 + 1), q_pos0)

    l_tot = jnp.sum(l_ref[...], axis=-1, keepdims=True)
    o_ref[0] = (acc_ref[...] / l_tot).astype(o_ref.dtype).reshape(heads, Q_BLOCK, kv_lora)


def _mla_prompt_attn(q, rows_bf, kv_lora, tk=1024):
    nb, heads, _, e = q.shape
    s = rows_bf.shape[0]
    tk = min(tk, s)
    assert s % tk == 0 and tk % Q_BLOCK == 0
    rows = heads * Q_BLOCK
    return pl.pallas_call(
        functools.partial(_mla_prompt_kernel, tk=tk, kv_lora=kv_lora),
        grid=(nb,),
        in_specs=[pl.BlockSpec((1, heads, Q_BLOCK, e), lambda i: (i, 0, 0, 0)), _const_spec((s, e))],
        out_specs=pl.BlockSpec((1, heads, Q_BLOCK, kv_lora), lambda i: (i, 0, 0, 0)),
        out_shape=jax.ShapeDtypeStruct((nb, heads, Q_BLOCK, kv_lora), BF16),
        scratch_shapes=[pltpu.VMEM((2, rows, tk), F32), pltpu.VMEM((rows, LANES), F32),
                        pltpu.VMEM((rows, LANES), F32), pltpu.VMEM((rows, kv_lora), F32)],
        compiler_params=_cparams(1, VMEM_LIMIT),
        name="mla_prompt_attn",
    )(q, rows_bf)


def _mla_sample_kernel(pt_ref, q_ref, new_ref, cache_ref, o_ref, pbuf, sem, kbuf_ref, nbuf_ref, m_ref, l_ref, acc_ref,
                       *, layer, n_group, dec_seq, kv_lora):
    p_idx = pl.program_id(1)
    n_steps = pl.num_programs(0) * pl.num_programs(1)
    t = pl.program_id(0) * pl.num_programs(1) + p_idx
    n_slots = pbuf.shape[0]
    ahead = n_slots - 1
    slot = lax.rem(t, n_slots)
    page = pbuf.shape[3]

    def page_copies(step, sl):
        return [pltpu.make_async_copy(cache_ref.at[layer, pt_ref[step * n_group + g]], pbuf.at[sl, g], sem.at[sl])
                for g in range(n_group)]

    @pl.when(t == 0)
    def _():
        for step in range(ahead):
            @pl.when(step < n_steps)
            def _():
                for c in page_copies(step, step):
                    c.start()

    @pl.when(t + ahead < n_steps)
    def _():
        for c in page_copies(t + ahead, lax.rem(t + ahead, n_slots)):
            c.start()

    q = q_ref[...]

    @pl.when(p_idx == 0)
    def _():
        nbuf_ref[...] = jnp.zeros(nbuf_ref.shape, F32)
        nbuf_ref[:dec_seq, :] = new_ref[...]
        nk = nbuf_ref[...].astype(BF16)
        s = _dot_nt(q, nk)
        tok = _irem(lax.broadcasted_iota(jnp.int32, s.shape, 0), dec_seq)
        s = jnp.where(lax.broadcasted_iota(jnp.int32, s.shape, 1) <= tok, s, NEG_INF)
        m0 = jnp.max(s, axis=-1, keepdims=True)
        p0 = jnp.exp2(s - m0)
        m_ref[...] = m0
        l_ref[...] = jnp.sum(p0, axis=-1, keepdims=True)
        acc_ref[...] = _dot(p0.astype(BF16), nk[:, :kv_lora])

    for c in page_copies(t, slot):
        c.wait()
    for g in range(n_group):
        kbuf_ref[:, g * page:(g + 1) * page] = pbuf[slot, g].astype(BF16)
    kt = kbuf_ref[...]
    s = _dot(q, kt)
    m_prev = m_ref[...]
    m_new = jnp.maximum(m_prev, jnp.max(s, axis=-1, keepdims=True))
    alpha = jnp.exp2(m_prev - m_new)
    p = jnp.exp2(s - m_new)
    l_ref[...] = alpha * l_ref[...] + jnp.sum(p, axis=-1, keepdims=True)
    acc_ref[...] = alpha * acc_ref[...] + _dot_nt(p.astype(BF16), kt[:kv_lora, :])
    m_ref[...] = m_new

    @pl.when(p_idx == pl.num_programs(1) - 1)
    def _():
        o_ref[...] = (acc_ref[...] / l_ref[...]).astype(o_ref.dtype)


def _mla_sample_attn(q, new_rows, cache_t, layer, page_table, kv_lora, n_group=32, n_slots=3):
    b, qrows, e = q.shape
    dec_seq = new_rows.shape[1]
    n_pages = page_table.shape[1]
    page = cache_t.shape[3]
    n_group = min(n_group, n_pages)
    assert n_pages % n_group == 0 and dec_seq <= page

    grid_spec = pltpu.PrefetchScalarGridSpec(
        num_scalar_prefetch=1,
        grid=(b, n_pages // n_group),
        in_specs=[pl.BlockSpec((None, qrows, e), lambda bi, pi, pt: (bi, 0, 0)),
                  pl.BlockSpec((None, dec_seq, e), lambda bi, pi, pt: (bi, 0, 0)),
                  pl.BlockSpec(memory_space=pl.ANY)],
        out_specs=pl.BlockSpec((None, qrows, kv_lora), lambda bi, pi, pt: (bi, 0, 0)),
        scratch_shapes=[pltpu.VMEM((n_slots, n_group, e, page), F32), pltpu.SemaphoreType.DMA((n_slots,)),
                        pltpu.VMEM((e, n_group * page), BF16), pltpu.VMEM((page, e), F32),
                        pltpu.VMEM((qrows, 1), F32), pltpu.VMEM((qrows, 1), F32), pltpu.VMEM((qrows, kv_lora), F32)],
    )
    return pl.pallas_call(
        functools.partial(_mla_sample_kernel, layer=layer, n_group=n_group, dec_seq=dec_seq, kv_lora=kv_lora),
        grid_spec=grid_spec,
        out_shape=jax.ShapeDtypeStruct((b, qrows, kv_lora), BF16),
        compiler_params=_cparams(2),
        name="mla_sample_attn",
    )(page_table.reshape(-1), q, new_rows, cache_t)


def _mla_out_kernel(o_ref, x_ref, wuv_ref, wo_ref, y_ref, ocat_scr):
    nblk, heads, tq, kv_lora = o_ref.shape
    v_dim = wuv_ref.shape[2]
    for hd in range(heads):
        o_h = _dot(o_ref[:, hd].reshape(nblk * tq, kv_lora), wuv_ref[hd])
        ocat_scr[:, hd * v_dim:(hd + 1) * v_dim] = o_h.astype(BF16)
    y_ref[...] = x_ref[...] + _dot(ocat_scr[...], wo_ref[...])


def _mla_out(o_lat, x, w_uv, w_o, rows_per_step=512):
    nb, heads, tq, kv_lora = o_lat.shape
    d = x.shape[1]
    assert o_lat.dtype == BF16
    nblk = max(1, min(nb, rows_per_step // tq))
    assert nb % nblk == 0
    return pl.pallas_call(
        _mla_out_kernel,
        grid=(nb // nblk,),
        in_specs=[pl.BlockSpec((nblk, heads, tq, kv_lora), lambda i: (i, 0, 0, 0)),
                  pl.BlockSpec((nblk * tq, d), lambda i: (i, 0)),
                  _const_spec(w_uv.shape), _const_spec(w_o.shape)],
        out_specs=pl.BlockSpec((nblk * tq, d), lambda i: (i, 0)),
        out_shape=jax.ShapeDtypeStruct(x.shape, F32),
        scratch_shapes=[pltpu.VMEM((nblk * tq, w_o.shape[0]), BF16)],
        compiler_params=_cparams(1),
        name="mla_out",
    )(o_lat, x, w_uv, w_o)


def _norm_linear_kernel(x_ref, g_ref, w_ref, b_ref, cos_ref, sin_ref, y_ref, *, rope_cols):
    h = _rms(x_ref[...], g_ref[...]).astype(BF16)
    y = _dot(h, w_ref[...]) + b_ref[...]
    cos_t, sin_t = cos_ref[...], sin_ref[...]
    for c in range(rope_cols // LANES):
        y_ref[:, c * LANES:(c + 1) * LANES] = _rope128(y[:, c * LANES:(c + 1) * LANES], cos_t, sin_t)
    if rope_cols < y.shape[1]:
        y_ref[:, rope_cols:] = y[:, rope_cols:]


def _norm_linear(x, g, w, b, cos_t, sin_t, rope_cols, tm=512):
    m, d = x.shape
    n = w.shape[1]
    tm = min(tm, m)
    row_spec = lambda width: pl.BlockSpec((tm, width), lambda i: (i, 0))
    return pl.pallas_call(
        functools.partial(_norm_linear_kernel, rope_cols=rope_cols),
        grid=(m // tm,),
        in_specs=[row_spec(d), _const_spec((1, d)), _const_spec((d, n)), _const_spec((1, n)),
                  row_spec(LANES), row_spec(LANES)],
        out_specs=row_spec(n),
        out_shape=jax.ShapeDtypeStruct((m, n), F32),
        compiler_params=_cparams(1),
        name="norm_linear_rope",
    )(x, g.reshape(1, d), w, b.reshape(1, n), cos_t, sin_t)


def _sink_attend(s, valid, sink, v):
    s = jnp.where(valid, s, NEG_INF)
    m = jnp.maximum(jnp.max(s, axis=-1, keepdims=True), sink)
    p = jnp.exp(s - m)
    pr = p / (jnp.sum(p, axis=-1, keepdims=True) + jnp.exp(sink - m))
    return _dot(pr.astype(BF16), v)


def _stack_group(q, kvh, group, hd):
    return jnp.concatenate([q[:, (kvh * group + g) * hd:(kvh * group + g + 1) * hd] for g in range(group)], axis=0)


def _sink_rows(sinks_ref, kvh, group, t):
    row = lax.broadcasted_iota(jnp.int32, (group * t, 1), 0)
    sink = jnp.full((group * t, 1), sinks_ref[kvh * group], F32)
    for g in range(1, group):
        sink = jnp.where(row >= g * t, sinks_ref[kvh * group + g], sink)
    return sink


def _swa_finish(o_scr, x_ref, wo_ref, bo_ref, y_ref):
    y_ref[...] = x_ref[...] + _dot(o_scr[...].astype(BF16), wo_ref[...]) + bo_ref[...]


def _swa_prompt_kernel(sinks_ref, q_ref, cur_ref, pk_ref, pv_ref, x_ref, wo_ref, bo_ref, y_ref, o_scr, s_scr, p_scr,
                       *, kv_heads, group, hd, scale):
    i = pl.program_id(0)
    t = Q_BLOCK
    nb = q_ref.shape[0] // t
    kw = kv_heads * hd
    shape = (group * t, 2 * t)
    r = _irem(lax.broadcasted_iota(jnp.int32, shape, 0), t)
    j = lax.broadcasted_iota(jnp.int32, shape, 1)
    band = (j > r) & (j <= r + t)
    band_first = band & ((j >= t) | (i > 0))
    kks, vvs = [], []
    for b in range(nb):
        r0, r1 = b * t, (b + 1) * t
        pk = pk_ref[...] if b == 0 else cur_ref[r0 - t:r0, :kw]
        pv = pv_ref[...] if b == 0 else cur_ref[r0 - t:r0, kw:]
        kks.append(jnp.concatenate([pk, cur_ref[r0:r1, :kw]], axis=0).astype(BF16))
        vvs.append(jnp.concatenate([pv, cur_ref[r0:r1, kw:]], axis=0).astype(BF16))
    for b in range(nb):
        q = q_ref[b * t:(b + 1) * t, :]
        for kvh in range(kv_heads):
            qg = _stack_group(q, kvh, group, hd).astype(BF16)
            s_scr[b * kv_heads + kvh] = _dot_nt(qg, kks[b][:, kvh * hd:(kvh + 1) * hd])
    for b in range(nb):
        for kvh in range(kv_heads):
            s = jnp.where(band_first if b == 0 else band, s_scr[b * kv_heads + kvh] * scale, NEG_INF)
            sink = _sink_rows(sinks_ref, kvh, group, t)
            m = jnp.maximum(jnp.max(s, axis=-1, keepdims=True), sink)
            p = jnp.exp(s - m)
            pr = p / (jnp.sum(p, axis=-1, keepdims=True) + jnp.exp(sink - m))
            p_scr[b * kv_heads + kvh] = pr.astype(BF16)
    for b in range(nb):
        for kvh in range(kv_heads):
            o = _dot(p_scr[b * kv_heads + kvh], vvs[b][:, kvh * hd:(kvh + 1) * hd])
            for g in range(group):
                hh = kvh * group + g
                o_scr[b * t:(b + 1) * t, hh * hd:(hh + 1) * hd] = o[g * t:(g + 1) * t]
    _swa_finish(o_scr, x_ref, wo_ref, bo_ref, y_ref)


def _swa_prompt_attn(q, kv, x, sinks, w_o, b_o, kv_heads, hd, scale, rows_per_step=512):
    m, d = x.shape
    qw = q.shape[1]
    kw = kv_heads * hd
    t = Q_BLOCK
    group = qw // kw
    tm = min(rows_per_step, m)
    nb = tm // t
    assert m % tm == 0 and tm % t == 0
    prev = lambda col: pl.BlockSpec((t, kw), lambda i: (jnp.maximum(i * nb - 1, 0), col))
    return pl.pallas_call(
        functools.partial(_swa_prompt_kernel, kv_heads=kv_heads, group=group, hd=hd, scale=scale),
        grid=(m // tm,),
        in_specs=[pl.BlockSpec(memory_space=pltpu.SMEM),
                  pl.BlockSpec((tm, qw), lambda i: (i, 0)),
                  pl.BlockSpec((tm, 2 * kw), lambda i: (i, 0)),
                  prev(0), prev(1),
                  pl.BlockSpec((tm, d), lambda i: (i, 0)),
                  _const_spec(w_o.shape), _const_spec((1, d))],
        out_specs=pl.BlockSpec((tm, d), lambda i: (i, 0)),
        out_shape=jax.ShapeDtypeStruct((m, d), F32),
        scratch_shapes=[pltpu.VMEM((tm, qw), F32), pltpu.VMEM((nb * kv_heads, group * t, 2 * t), F32),
                        pltpu.VMEM((nb * kv_heads, group * t, 2 * t), BF16)],
        compiler_params=_cparams(1),
        name="swa_prompt_attn",
    )(sinks, q, kv, kv, kv, x, w_o, b_o.reshape(1, d))


def _swa_sample_kernel(sinks_ref, q_ref, new_ref, ck_ref, cv_ref, x_ref, wo_ref, bo_ref, y_ref, o_scr,
                       *, kv_heads, group, hd, dec_seq, scale):
    n_seq, wb, kw = ck_ref.shape
    t = n_seq * dec_seq
    q = q_ref[...]
    kk = jnp.concatenate([ck_ref[...].reshape(n_seq * wb, kw), new_ref[:, :kw]], axis=0).astype(BF16)
    vv = jnp.concatenate([cv_ref[...].reshape(n_seq * wb, kw), new_ref[:, kw:]], axis=0).astype(BF16)
    n_old = n_seq * wb
    shape = (group * t, n_old + t)
    r = _irem(lax.broadcasted_iota(jnp.int32, shape, 0), t)
    r_seq, r_tok = _idiv(r, dec_seq), _irem(r, dec_seq)
    j = lax.broadcasted_iota(jnp.int32, shape, 1)
    old = j < n_old
    j_new = jnp.maximum(j - n_old, 0)
    j_seq = jnp.where(old, _idiv(j, wb), _idiv(j_new, dec_seq))
    j_pos = jnp.where(old, _irem(j, wb), wb + _irem(j_new, dec_seq))
    valid = (j_seq == r_seq) & (j_pos > r_tok) & (j_pos <= r_tok + wb)
    for kvh in range(kv_heads):
        qg = _stack_group(q, kvh, group, hd).astype(BF16)
        s = _dot_nt(qg, kk[:, kvh * hd:(kvh + 1) * hd]) * scale
        o = _sink_attend(s, valid, _sink_rows(sinks_ref, kvh, group, t), vv[:, kvh * hd:(kvh + 1) * hd])
        for g in range(group):
            hh = kvh * group + g
            o_scr[:, hh * hd:(hh + 1) * hd] = o[g * t:(g + 1) * t]
    _swa_finish(o_scr, x_ref, wo_ref, bo_ref, y_ref)


def _swa_sample_attn(q, kv_new, cache_k, cache_v, x, sinks, w_o, b_o, kv_heads, hd, dec_seq, scale, n_seq=8):
    m, d = x.shape
    qw = q.shape[1]
    kw = kv_heads * hd
    b, wb = cache_k.shape[:2]
    n_seq = min(n_seq, b)
    t = n_seq * dec_seq
    group = qw // kw
    return pl.pallas_call(
        functools.partial(_swa_sample_kernel, kv_heads=kv_heads, group=group, hd=hd, dec_seq=dec_seq, scale=scale),
        grid=(b // n_seq,),
        in_specs=[pl.BlockSpec(memory_space=pltpu.SMEM),
                  pl.BlockSpec((t, qw), lambda i: (i, 0)),
                  pl.BlockSpec((t, 2 * kw), lambda i: (i, 0)),
                  pl.BlockSpec((n_seq, wb, kw), lambda i: (i, 0, 0)),
                  pl.BlockSpec((n_seq, wb, kw), lambda i: (i, 0, 0)),
                  pl.BlockSpec((t, d), lambda i: (i, 0)),
                  _const_spec(w_o.shape), _const_spec((1, d))],
        out_specs=pl.BlockSpec((t, d), lambda i: (i, 0)),
        out_shape=jax.ShapeDtypeStruct((m, d), F32),
        scratch_shapes=[pltpu.VMEM((t, qw), F32)],
        compiler_params=_cparams(1),
        name="swa_sample_attn",
    )(sinks, q, kv_new, cache_k.reshape(b, wb, kw), cache_v.reshape(b, wb, kw), x, w_o, b_o.reshape(1, d))


def _prep_weights(p):
    n_a, kv_lora, heads, nope = p["mla_w_uk"].shape
    q_lora = p["mla_w_dq"].shape[2]
    w = {"ffn_in": p["ffn_w_in"].astype(BF16), "ffn_out": p["ffn_w_out"].astype(BF16), "mla": []}
    for l in range(n_a):
        w_uq = p["mla_w_uq"][l].reshape(q_lora, heads, nope + ROPE_DIM)
        w_dkv = p["mla_w_dkv"][l]
        w["mla"].append({
            "norm": p["mla_norm"][l], "q_norm": p["mla_q_norm"][l], "kv_norm": p["mla_kv_norm"][l],
            "w_dq": p["mla_w_dq"][l].astype(BF16),
            "w_uq_n": w_uq[:, :, :nope].reshape(q_lora, heads * nope).astype(BF16),
            "w_uq_r": w_uq[:, :, nope:].reshape(q_lora, heads * ROPE_DIM).astype(BF16),
            "w_uk": jnp.transpose(p["mla_w_uk"][l], (1, 2, 0)).astype(BF16),
            "w_dkv_c": w_dkv[:, :kv_lora].astype(BF16),
            "w_dkv_r": jnp.pad(w_dkv[:, kv_lora:], ((0, 0), (0, LANES - ROPE_DIM))).astype(BF16),
            "w_uv": jnp.transpose(p["mla_w_uv"][l], (1, 0, 2)).astype(BF16),
            "w_o": p["mla_w_o"][l].astype(BF16),
        })
    w["swa_kv"] = p["swa_w_kv"].astype(BF16)
    w["swa_q"] = p["swa_w_q"].astype(BF16)
    w["swa_o"] = p["swa_w_o"].astype(BF16)
    return w


def _run_trunk(x, pos, p, w, mla_q_scale, mla_attend, swa_attend):
    depth = p["ffn_norm"].shape[0]
    n_a = p["mla_norm"].shape[0]
    kv_w = w["swa_kv"].shape[1]
    cos_t, sin_t = _rope_tables(pos)
    rows_all, kv = [], None
    for l in range(depth):
        if l == n_a:
            kv = _norm_linear(x, p["kv_norm"], w["swa_kv"], p["swa_b_kv"], cos_t, sin_t, rope_cols=kv_w // 2)
        x = _ffn(x, p["ffn_norm"][l, 0], w["ffn_in"][l, 0], w["ffn_out"][l, 0])
        if l < n_a:
            wl = w["mla"][l]
            q, rows, rows_bf = _mla_proj(x, wl, cos_t, sin_t, mla_q_scale)
            o_lat = mla_attend(l, q, rows, rows_bf)
            x = _mla_out(o_lat, x, wl["w_uv"], wl["w_o"])
            rows_all.append(rows)
        else:
            jl = l - n_a
            q = _norm_linear(x, p["swa_norm"][jl], w["swa_q"][jl], p["swa_b_q"][jl], cos_t, sin_t,
                             rope_cols=w["swa_q"].shape[2])
            x = swa_attend(q, kv, x, p["swa_sinks"][jl], w["swa_o"][jl], p["swa_b_o"][jl])
        x = _ffn(x, p["ffn_norm"][l, 1], w["ffn_in"][l, 1], w["ffn_out"][l, 1],
                 final_g=p["final_norm"] if l == depth - 1 else None)
    return x, jnp.stack(rows_all, axis=0), kv


def kernel(x_prompt, x_sample, cache_mla, cache_swa_k, cache_swa_v, page_table, ffn_norm, ffn_w_in, ffn_w_out, mla_norm, mla_w_dq, mla_q_norm, mla_w_uq, mla_w_dkv, mla_kv_norm, mla_w_uk, mla_w_uv, mla_w_o, kv_norm, swa_w_kv, swa_b_kv, swa_norm, swa_w_q, swa_b_q, swa_sinks, swa_w_o, swa_b_o, final_norm):
    p = {
        "ffn_norm": ffn_norm, "ffn_w_in": ffn_w_in, "ffn_w_out": ffn_w_out,
        "mla_norm": mla_norm, "mla_w_dq": mla_w_dq, "mla_q_norm": mla_q_norm, "mla_w_uq": mla_w_uq,
        "mla_w_dkv": mla_w_dkv, "mla_kv_norm": mla_kv_norm, "mla_w_uk": mla_w_uk, "mla_w_uv": mla_w_uv,
        "mla_w_o": mla_w_o,
        "kv_norm": kv_norm, "swa_w_kv": swa_w_kv, "swa_b_kv": swa_b_kv,
        "swa_norm": swa_norm, "swa_w_q": swa_w_q, "swa_b_q": swa_b_q, "swa_sinks": swa_sinks,
        "swa_w_o": swa_w_o, "swa_b_o": swa_b_o, "final_norm": final_norm,
    }
    batch, seq, d = x_prompt.shape
    dec_batch, dec_seq, _ = x_sample.shape
    _, kv_lora, heads, nope = mla_w_uk.shape
    e = kv_lora + ROPE_DIM
    kv_heads, hd = cache_swa_k.shape[2:]
    past_len = page_table.shape[1] * cache_mla.shape[2]
    mla_q_scale = float(nope + ROPE_DIM) ** -0.5 * LOG2_E
    swa_scale = float(hd) ** -0.5
    cache_t = jnp.swapaxes(cache_mla, 2, 3)
    assert batch == 1 and seq % Q_BLOCK == 0 and cache_swa_k.shape[1] == Q_BLOCK
    w = _prep_weights(p)

    def mla_prompt(l, q, rows, rows_bf):
        return _mla_prompt_attn(q, rows_bf, kv_lora)

    def swa_prompt(q, kv, x, sinks, w_o, b_o):
        return _swa_prompt_attn(q, kv, x, sinks, w_o, b_o, kv_heads, hd, swa_scale)

    y_p, rows_p, kv_p = _run_trunk(x_prompt.reshape(seq, d), jnp.arange(seq, dtype=jnp.int32), p, w, mla_q_scale,
                                   mla_prompt, swa_prompt)

    n_tok = dec_batch * dec_seq

    def mla_sample(l, q, rows, rows_bf):
        q_tok = jnp.transpose(q, (0, 2, 1, 3)).reshape(dec_batch, dec_seq, heads, e)
        q_seq = jnp.transpose(q_tok, (0, 2, 1, 3)).reshape(dec_batch, heads * dec_seq, e)
        o = _mla_sample_attn(q_seq, rows.reshape(dec_batch, dec_seq, e), cache_t, l, page_table, kv_lora)
        o = jnp.transpose(o.reshape(dec_batch, heads, dec_seq, kv_lora), (1, 0, 2, 3))
        return o.reshape(1, heads, n_tok, kv_lora)

    def swa_sample(q, kv, x, sinks, w_o, b_o):
        return _swa_sample_attn(q, kv, cache_swa_k, cache_swa_v, x, sinks, w_o, b_o, kv_heads, hd, dec_seq, swa_scale)

    pos_s = jnp.tile(past_len + jnp.arange(dec_seq, dtype=jnp.int32), dec_batch)
    y_s, rows_s, kv_s = _run_trunk(x_sample.reshape(n_tok, d), pos_s, p, w, mla_q_scale, mla_sample, swa_sample)

    kw = kv_heads * hd
    w_p = min(Q_BLOCK, seq)
    k_p = kv_p[seq - w_p:, :kw].reshape(1, w_p, kv_heads, hd)
    v_p = kv_p[seq - w_p:, kw:].reshape(1, w_p, kv_heads, hd)
    k_s = kv_s[:, :kw].reshape(dec_batch, dec_seq, kv_heads, hd)
    v_s = kv_s[:, kw:].reshape(dec_batch, dec_seq, kv_heads, hd)
    swa_k_sample = jnp.concatenate([cache_swa_k, k_s], axis=1)[:, dec_seq:]
    swa_v_sample = jnp.concatenate([cache_swa_v, v_s], axis=1)[:, dec_seq:]
    return (y_p.reshape(1, seq, d), y_s.reshape(dec_batch, dec_seq, d),
            rows_p.reshape(-1, 1, seq, e), rows_s.reshape(-1, dec_batch, dec_seq, e),
            k_p, v_p, swa_k_sample, swa_v_sample)
```

```python
import functools

import jax
import jax.numpy as jnp
from jax import lax
from jax.experimental import pallas as pl
from jax.experimental.pallas import tpu as pltpu

F32 = jnp.float32
BF16 = jnp.bfloat16

EPS = 1e-6
ROPE_THETA = 10000.0
FFN_RES_WEIGHT = 0.5
LANES = 128
MXU_DIM = 256
ROPE_DIM = 64
Q_BLOCK = 128
VMEM_LIMIT = 56 * 1024 * 1024
NEG_INF = float("-inf")
LOG2_E = 1.4426950408889634


def _cparams(n_axes, vmem=None):
    return pltpu.CompilerParams(dimension_semantics=("arbitrary",) * n_axes, vmem_limit_bytes=vmem)


def _const_spec(shape):
    nd = len(shape)
    return pl.BlockSpec(shape, lambda *_: (0,) * nd, pipeline_mode=pl.Buffered(1))


def _dot(a, b):
    return jnp.dot(a, b, preferred_element_type=F32)


def _dot_nt(a, b):
    return lax.dot_general(a, b, (((1,), (1,)), ((), ())), preferred_element_type=F32)


def _idiv(x, n):
    if n & (n - 1) == 0:
        return x >> (n.bit_length() - 1)
    return lax.div(x, jnp.full(x.shape, n, x.dtype))


def _irem(x, n):
    if n & (n - 1) == 0:
        return x & (n - 1)
    return lax.rem(x, jnp.full(x.shape, n, x.dtype))


def _rms(x, g):
    return x * lax.rsqrt(jnp.mean(x * x, axis=-1, keepdims=True) + EPS) * g


def _rope128(x, cos_t, sin_t):
    lane = lax.broadcasted_iota(jnp.int32, x.shape, 1)
    first_half = (lane & (ROPE_DIM // 2)) == 0
    partner = jnp.where(first_half, pltpu.roll(x, LANES - ROPE_DIM // 2, 1), pltpu.roll(x, ROPE_DIM // 2, 1))
    return x * cos_t + partner * sin_t


def _rope_tables(pos):
    half = ROPE_DIM // 2
    inv_freq = jnp.exp(-(jnp.arange(half, dtype=F32) / half) * jnp.log(F32(ROPE_THETA)))
    ang = pos.astype(F32)[:, None] * inv_freq[None, :]
    cos, sin = jnp.cos(ang), jnp.sin(ang)
    reps = LANES // ROPE_DIM
    cos_t = jnp.tile(cos, (1, 2 * reps))
    sin_t = jnp.tile(jnp.concatenate([-sin, sin], axis=1), (1, reps))
    return cos_t, sin_t


def _ffn_kernel(x_ref, g_ref, win_ref, wout_ref, *rest, d_ff, bounds, final):
    if final:
        gf_ref, o_ref = rest
    else:
        (o_ref,) = rest
    x = x_ref[...]
    h = _rms(x, g_ref[...]).astype(BF16)
    acc = None
    for c0, c1 in zip(bounds[:-1], bounds[1:]):
        gate = _dot(h, win_ref[:, c0:c1])
        up = _dot(h, win_ref[:, d_ff + c0:d_ff + c1])
        act = (gate * (1.0 / (1.0 + jnp.exp(-gate))) * up).astype(BF16)
        part = _dot(act, wout_ref[c0:c1, :])
        acc = part if acc is None else acc + part
    y = x + FFN_RES_WEIGHT * acc
    if final:
        y = _rms(y, gf_ref[...])
    o_ref[...] = y


def _ffn(x, g, w_in, w_out, final_g=None, tm=512):
    m, d = x.shape
    d_ff = w_out.shape[0]
    tm = min(tm, m)
    unit = MXU_DIM if d_ff % MXU_DIM == 0 else LANES
    n_units = d_ff // unit
    bounds = (0, (n_units + 1) // 2 * unit, d_ff) if n_units > 1 and d_ff % unit == 0 else (0, d_ff)
    in_specs = [
        pl.BlockSpec((tm, d), lambda i: (i, 0)),
        _const_spec((1, d)),
        _const_spec((d, 2 * d_ff)),
        _const_spec((d_ff, d)),
    ]
    args = [x, g.reshape(1, d), w_in, w_out]
    if final_g is not None:
        in_specs.append(_const_spec((1, d)))
        args.append(final_g.reshape(1, d))
    return pl.pallas_call(
        functools.partial(_ffn_kernel, d_ff=d_ff, bounds=bounds, final=final_g is not None),
        grid=(m // tm,),
        in_specs=in_specs,
        out_specs=pl.BlockSpec((tm, d), lambda i: (i, 0)),
        out_shape=jax.ShapeDtypeStruct((m, d), F32),
        compiler_params=_cparams(1, VMEM_LIMIT),
        name="ffn_half",
    )(*args)


def _mla_proj_kernel(x_ref, g_ref, wdq_ref, gq_ref, wuqn_ref, wuqr_ref, wuk_ref, wdkvc_ref, wdkvr_ref,
                     gkv_ref, cos_ref, sin_ref, q_ref, rows_ref, rowsb_ref, *, heads, nope, kv_lora, q_scale):
    tm = x_ref.shape[0]
    cos_t, sin_t = cos_ref[...], sin_ref[...]
    h = _rms(x_ref[...], g_ref[...]).astype(BF16)
    cq = _rms(_dot(h, wdq_ref[...]), gq_ref[...]).astype(BF16)
    qn = _dot(cq, wuqn_ref[...])
    qr = _dot(cq, wuqr_ref[...])
    qr = jnp.concatenate(
        [_rope128(qr[:, c * LANES:(c + 1) * LANES], cos_t, sin_t) for c in range(qr.shape[1] // LANES)], axis=1)
    c_lat = _rms(_dot(h, wdkvc_ref[...]), gkv_ref[...])
    k_pe = _rope128(_dot(h, wdkvr_ref[...]), cos_t, sin_t)[:, :ROPE_DIM]
    rows_ref[:, :kv_lora] = c_lat
    rows_ref[:, kv_lora:] = k_pe
    rowsb_ref[:, :kv_lora] = c_lat.astype(BF16)
    rowsb_ref[:, kv_lora:] = k_pe.astype(BF16)
    for hd in range(heads):
        q_lat = (_dot(qn[:, hd * nope:(hd + 1) * nope].astype(BF16), wuk_ref[hd]) * q_scale).astype(BF16)
        q_pe = (qr[:, hd * ROPE_DIM:(hd + 1) * ROPE_DIM] * q_scale).astype(BF16)
        for r in range(tm // Q_BLOCK):
            q_ref[r, hd, :, :kv_lora] = q_lat[r * Q_BLOCK:(r + 1) * Q_BLOCK]
            q_ref[r, hd, :, kv_lora:] = q_pe[r * Q_BLOCK:(r + 1) * Q_BLOCK]


def _mla_proj(x, w, cos_t, sin_t, q_scale, tm=512):
    m, d = x.shape
    heads, nope, kv_lora = w["w_uk"].shape
    e = kv_lora + ROPE_DIM
    q_lora = w["w_dq"].shape[1]
    tm = min(tm, m)
    nb = tm // Q_BLOCK
    consts = [w["norm"].reshape(1, d), w["w_dq"], w["q_norm"].reshape(1, q_lora), w["w_uq_n"], w["w_uq_r"],
              w["w_uk"], w["w_dkv_c"], w["w_dkv_r"], w["kv_norm"].reshape(1, kv_lora)]
    row_spec = lambda width: pl.BlockSpec((tm, width), lambda i: (i, 0))
    return pl.pallas_call(
        functools.partial(_mla_proj_kernel, heads=heads, nope=nope, kv_lora=kv_lora, q_scale=q_scale),
        grid=(m // tm,),
        in_specs=[row_spec(d)] + [_const_spec(c.shape) for c in consts] + [row_spec(LANES), row_spec(LANES)],
        out_specs=[pl.BlockSpec((nb, heads, Q_BLOCK, e), lambda i: (i, 0, 0, 0)), row_spec(e), row_spec(e)],
        out_shape=[jax.ShapeDtypeStruct((m // Q_BLOCK, heads, Q_BLOCK, e), BF16),
                   jax.ShapeDtypeStruct((m, e), F32),
                   jax.ShapeDtypeStruct((m, e), BF16)],
        compiler_params=_cparams(1),
        name="mla_proj",
    )(x, *consts, cos_t, sin_t)


def _flash_scores(q_ref, k_ref, s_scr, slot, start, *, tk):
    heads, _, e = q_ref.shape[1:]
    s_scr[slot] = _dot_nt(q_ref[0].reshape(heads * Q_BLOCK, e), k_ref[pl.ds(start, tk), :])


def _flash_update(k_ref, s_scr, slot, start, q_pos0, m_ref, l_ref, acc_ref, *, tk, kv_lora):
    s = s_scr[slot]
    if q_pos0 is not None:
        q_pos = q_pos0 + (lax.broadcasted_iota(jnp.int32, s.shape, 0) & (Q_BLOCK - 1))
        k_pos = start + lax.broadcasted_iota(jnp.int32, s.shape, 1)
        s = jnp.where(k_pos <= q_pos, s, NEG_INF)
    chunks = [s[:, j * LANES:(j + 1) * LANES] for j in range(tk // LANES)]
    m_prev = m_ref[...]
    m_new = jnp.maximum(m_prev, jnp.max(functools.reduce(jnp.maximum, chunks), axis=-1, keepdims=True))
    alpha = jnp.exp2(m_prev - m_new)
    ps = [jnp.exp2(ch - m_new) for ch in chunks]
    l_ref[...] = alpha * l_ref[...] + functools.reduce(jnp.add, ps)
    pv = _dot(jnp.concatenate(ps, axis=1).astype(BF16), k_ref[pl.ds(start, tk), :kv_lora])
    acc_ref[...] = acc_ref[...] * jnp.concatenate([alpha] * (kv_lora // LANES), axis=1) + pv
    m_ref[...] = m_new


def _mla_prompt_kernel(q_ref, k_ref, o_ref, s_scr, m_ref, l_ref, acc_ref, *, tk, kv_lora):
    i = pl.program_id(0)
    heads = q_ref.shape[1]
    m_ref[...] = jnp.full(m_ref.shape, NEG_INF, F32)
    l_ref[...] = jnp.zeros(l_ref.shape, F32)
    acc_ref[...] = jnp.zeros(acc_ref.shape, F32)
    scores = functools.partial(_flash_scores, q_ref, k_ref, s_scr, tk=tk)
    update = functools.partial(_flash_update, k_ref, s_scr, m_ref=m_ref, l_ref=l_ref, acc_ref=acc_ref,
                               tk=tk, kv_lora=kv_lora)
    at = lambda j: pl.multiple_of(j * tk, tk)
    n_full = (i * Q_BLOCK) // tk
    n_pairs = n_full // 2
    scores(0, at(0))

    def body(p, carry):
        j = 2 * p
        scores(1, at(j + 1))
        update(0, at(j), None)
        scores(0, at(j + 2))
        update(1, at(j + 1), None)
        return carry

    lax.fori_loop(0, n_pairs, body, 0)
    j0 = 2 * n_pairs
    q_pos0 = i * Q_BLOCK

    @pl.when(n_full == j0)
    def _():
        update(0, at(j0), q_pos0)

    @pl.when(n_full != j0)
    def _():
        scores(1, at(j0 + 1))
        update(0, at(j0), None)
        update(1, at(j0 + 1), q_pos0)

    l_tot = jnp.sum(l_ref[...], axis=-1, keepdims=True)
    o_ref[0] = (acc_ref[...] / l_tot).astype(o_ref.dtype).reshape(heads, Q_BLOCK, kv_lora)


def _mla_prompt_attn(q, rows_bf, kv_lora, tk=1024):
    nb, heads, _, e = q.shape
    s = rows_bf.shape[0]
    tk = min(tk, s)
    assert s % tk == 0 and tk % Q_BLOCK == 0
    rows = heads * Q_BLOCK
    return pl.pallas_call(
        functools.partial(_mla_prompt_kernel, tk=tk, kv_lora=kv_lora),
        grid=(nb,),
        in_specs=[pl.BlockSpec((1, heads, Q_BLOCK, e), lambda i: (i, 0, 0, 0)), _const_spec((s, e))],
        out_specs=pl.BlockSpec((1, heads, Q_BLOCK, kv_lora), lambda i: (i, 0, 0, 0)),
        out_shape=jax.ShapeDtypeStruct((nb, heads, Q_BLOCK, kv_lora), BF16),
        scratch_shapes=[pltpu.VMEM((2, rows, tk), F32), pltpu.VMEM((rows, LANES), F32),
                        pltpu.VMEM((rows, LANES), F32), pltpu.VMEM((rows, kv_lora), F32)],
        compiler_params=_cparams(1, VMEM_LIMIT),
        name="mla_prompt_attn",
    )(q, rows_bf)


def _mla_sample_kernel(pt_ref, q_ref, new_ref, cache_ref, o_ref, pbuf, sem, kbuf_ref, nbuf_ref, m_ref, l_ref, acc_ref,
                       *, layer, n_group, dec_seq, kv_lora):
    p_idx = pl.program_id(1)
    n_steps = pl.num_programs(0) * pl.num_programs(1)
    t = pl.program_id(0) * pl.num_programs(1) + p_idx
    n_slots = pbuf.shape[0]
    ahead = n_slots - 1
    slot = lax.rem(t, n_slots)
    page = pbuf.shape[3]

    def page_copies(step, sl):
        return [pltpu.make_async_copy(cache_ref.at[layer, pt_ref[step * n_group + g]], pbuf.at[sl, g], sem.at[sl])
                for g in range(n_group)]

    @pl.when(t == 0)
    def _():
        for step in range(ahead):
            @pl.when(step < n_steps)
            def _():
                for c in page_copies(step, step):
                    c.start()

    @pl.when(t + ahead < n_steps)
    def _():
        for c in page_copies(t + ahead, lax.rem(t + ahead, n_slots)):
            c.start()

    q = q_ref[...]

    @pl.when(p_idx == 0)
    def _():
        nbuf_ref[...] = jnp.zeros(nbuf_ref.shape, F32)
        nbuf_ref[:dec_seq, :] = new_ref[...]
        nk = nbuf_ref[...].astype(BF16)
        s = _dot_nt(q, nk)
        tok = _irem(lax.broadcasted_iota(jnp.int32, s.shape, 0), dec_seq)
        s = jnp.where(lax.broadcasted_iota(jnp.int32, s.shape, 1) <= tok, s, NEG_INF)
        m0 = jnp.max(s, axis=-1, keepdims=True)
        p0 = jnp.exp2(s - m0)
        m_ref[...] = m0
        l_ref[...] = jnp.sum(p0, axis=-1, keepdims=True)
        acc_ref[...] = _dot(p0.astype(BF16), nk[:, :kv_lora])

    for c in page_copies(t, slot):
        c.wait()
    for g in range(n_group):
        kbuf_ref[:, g * page:(g + 1) * page] = pbuf[slot, g].astype(BF16)
    kt = kbuf_ref[...]
    s = _dot(q, kt)
    m_prev = m_ref[...]
    m_new = jnp.maximum(m_prev, jnp.max(s, axis=-1, keepdims=True))
    alpha = jnp.exp2(m_prev - m_new)
    p = jnp.exp2(s - m_new)
    l_ref[...] = alpha * l_ref[...] + jnp.sum(p, axis=-1, keepdims=True)
    acc_ref[...] = alpha * acc_ref[...] + _dot_nt(p.astype(BF16), kt[:kv_lora, :])
    m_ref[...] = m_new

    @pl.when(p_idx == pl.num_programs(1) - 1)
    def _():
        o_ref[...] = (acc_ref[...] / l_ref[...]).astype(o_ref.dtype)


def _mla_sample_attn(q, new_rows, cache_t, layer, page_table, kv_lora, n_group=64, n_slots=3):
    b, qrows, e = q.shape
    dec_seq = new_rows.shape[1]
    n_pages = page_table.shape[1]
    page = cache_t.shape[3]
    n_group = min(n_group, n_pages)
    assert n_pages % n_group == 0 and dec_seq <= page

    grid_spec = pltpu.PrefetchScalarGridSpec(
        num_scalar_prefetch=1,
        grid=(b, n_pages // n_group),
        in_specs=[pl.BlockSpec((None, qrows, e), lambda bi, pi, pt: (bi, 0, 0)),
                  pl.BlockSpec((None, dec_seq, e), lambda bi, pi, pt: (bi, 0, 0)),
                  pl.BlockSpec(memory_space=pl.ANY)],
        out_specs=pl.BlockSpec((None, qrows, kv_lora), lambda bi, pi, pt: (bi, 0, 0)),
        scratch_shapes=[pltpu.VMEM((n_slots, n_group, e, page), F32), pltpu.SemaphoreType.DMA((n_slots,)),
                        pltpu.VMEM((e, n_group * page), BF16), pltpu.VMEM((page, e), F32),
                        pltpu.VMEM((qrows, 1), F32), pltpu.VMEM((qrows, 1), F32), pltpu.VMEM((qrows, kv_lora), F32)],
    )
    return pl.pallas_call(
        functools.partial(_mla_sample_kernel, layer=layer, n_group=n_group, dec_seq=dec_seq, kv_lora=kv_lora),
        grid_spec=grid_spec,
        out_shape=jax.ShapeDtypeStruct((b, qrows, kv_lora), BF16),
        compiler_params=_cparams(2),
        name="mla_sample_attn",
    )(page_table.reshape(-1), q, new_rows, cache_t)


def _mla_out_kernel(o_ref, x_ref, wuv_ref, wo_ref, y_ref, ocat_scr):
    nblk, heads, tq, kv_lora = o_ref.shape
    v_dim = wuv_ref.shape[2]
    for hd in range(heads):
        o_h = _dot(o_ref[:, hd].reshape(nblk * tq, kv_lora), wuv_ref[hd])
        ocat_scr[:, hd * v_dim:(hd + 1) * v_dim] = o_h.astype(BF16)
    y_ref[...] = x_ref[...] + _dot(ocat_scr[...], wo_ref[...])


def _mla_out(o_lat, x, w_uv, w_o, rows_per_step=512):
    nb, heads, tq, kv_lora = o_lat.shape
    d = x.shape[1]
    assert o_lat.dtype == BF16
    nblk = max(1, min(nb, rows_per_step // tq))
    assert nb % nblk == 0
    return pl.pallas_call(
        _mla_out_kernel,
        grid=(nb // nblk,),
        in_specs=[pl.BlockSpec((nblk, heads, tq, kv_lora), lambda i: (i, 0, 0, 0)),
                  pl.BlockSpec((nblk * tq, d), lambda i: (i, 0)),
                  _const_spec(w_uv.shape), _const_spec(w_o.shape)],
        out_specs=pl.BlockSpec((nblk * tq, d), lambda i: (i, 0)),
        out_shape=jax.ShapeDtypeStruct(x.shape, F32),
        scratch_shapes=[pltpu.VMEM((nblk * tq, w_o.shape[0]), BF16)],
        compiler_params=_cparams(1),
        name="mla_out",
    )(o_lat, x, w_uv, w_o)


def _norm_linear_kernel(x_ref, g_ref, w_ref, b_ref, cos_ref, sin_ref, y_ref, *, rope_cols):
    h = _rms(x_ref[...], g_ref[...]).astype(BF16)
    y = _dot(h, w_ref[...]) + b_ref[...]
    cos_t, sin_t = cos_ref[...], sin_ref[...]
    for c in range(rope_cols // LANES):
        y_ref[:, c * LANES:(c + 1) * LANES] = _rope128(y[:, c * LANES:(c + 1) * LANES], cos_t, sin_t)
    if rope_cols < y.shape[1]:
        y_ref[:, rope_cols:] = y[:, rope_cols:]


def _norm_linear(x, g, w, b, cos_t, sin_t, rope_cols, tm=512):
    m, d = x.shape
    n = w.shape[1]
    tm = min(tm, m)
    row_spec = lambda width: pl.BlockSpec((tm, width), lambda i: (i, 0))
    return pl.pallas_call(
        functools.partial(_norm_linear_kernel, rope_cols=rope_cols),
        grid=(m // tm,),
        in_specs=[row_spec(d), _const_spec((1, d)), _const_spec((d, n)), _const_spec((1, n)),
                  row_spec(LANES), row_spec(LANES)],
        out_specs=row_spec(n),
        out_shape=jax.ShapeDtypeStruct((m, n), F32),
        compiler_params=_cparams(1),
        name="norm_linear_rope",
    )(x, g.reshape(1, d), w, b.reshape(1, n), cos_t, sin_t)


def _sink_attend(s, valid, sink, v):
    s = jnp.where(valid, s, NEG_INF)
    m = jnp.maximum(jnp.max(s, axis=-1, keepdims=True), sink)
    p = jnp.exp(s - m)
    pr = p / (jnp.sum(p, axis=-1, keepdims=True) + jnp.exp(sink - m))
    return _dot(pr.astype(BF16), v)


def _stack_group(q, kvh, group, hd):
    return jnp.concatenate([q[:, (kvh * group + g) * hd:(kvh * group + g + 1) * hd] for g in range(group)], axis=0)


def _sink_rows(sinks_ref, kvh, group, t):
    row = lax.broadcasted_iota(jnp.int32, (group * t, 1), 0)
    sink = jnp.full((group * t, 1), sinks_ref[kvh * group], F32)
    for g in range(1, group):
        sink = jnp.where(row >= g * t, sinks_ref[kvh * group + g], sink)
    return sink


def _swa_finish(o_scr, x_ref, wo_ref, bo_ref, y_ref):
    y_ref[...] = x_ref[...] + _dot(o_scr[...].astype(BF16), wo_ref[...]) + bo_ref[...]


def _swa_prompt_kernel(sinks_ref, q_ref, cur_ref, pk_ref, pv_ref, x_ref, wo_ref, bo_ref, y_ref, o_scr, s_scr, p_scr,
                       *, kv_heads, group, hd, scale):
    i = pl.program_id(0)
    t = Q_BLOCK
    nb = q_ref.shape[0] // t
    kw = kv_heads * hd
    shape = (group * t, 2 * t)
    r = _irem(lax.broadcasted_iota(jnp.int32, shape, 0), t)
    j = lax.broadcasted_iota(jnp.int32, shape, 1)
    band = (j > r) & (j <= r + t)
    band_first = band & ((j >= t) | (i > 0))
    kks, vvs = [], []
    for b in range(nb):
        r0, r1 = b * t, (b + 1) * t
        pk = pk_ref[...] if b == 0 else cur_ref[r0 - t:r0, :kw]
        pv = pv_ref[...] if b == 0 else cur_ref[r0 - t:r0, kw:]
        kks.append(jnp.concatenate([pk, cur_ref[r0:r1, :kw]], axis=0).astype(BF16))
        vvs.append(jnp.concatenate([pv, cur_ref[r0:r1, kw:]], axis=0).astype(BF16))
    for b in range(nb):
        q = q_ref[b * t:(b + 1) * t, :]
        for kvh in range(kv_heads):
            qg = _stack_group(q, kvh, group, hd).astype(BF16)
            s_scr[b * kv_heads + kvh] = _dot_nt(qg, kks[b][:, kvh * hd:(kvh + 1) * hd])
    for b in range(nb):
        for kvh in range(kv_heads):
            s = jnp.where(band_first if b == 0 else band, s_scr[b * kv_heads + kvh] * scale, NEG_INF)
            sink = _sink_rows(sinks_ref, kvh, group, t)
            m = jnp.maximum(jnp.max(s, axis=-1, keepdims=True), sink)
            p = jnp.exp(s - m)
            pr = p / (jnp.sum(p, axis=-1, keepdims=True) + jnp.exp(sink - m))
            p_scr[b * kv_heads + kvh] = pr.astype(BF16)
    for b in range(nb):
        for kvh in range(kv_heads):
            o = _dot(p_scr[b * kv_heads + kvh], vvs[b][:, kvh * hd:(kvh + 1) * hd])
            for g in range(group):
                hh = kvh * group + g
                o_scr[b * t:(b + 1) * t, hh * hd:(hh + 1) * hd] = o[g * t:(g + 1) * t]
    _swa_finish(o_scr, x_ref, wo_ref, bo_ref, y_ref)


def _swa_prompt_attn(q, kv, x, sinks, w_o, b_o, kv_heads, hd, scale, rows_per_step=512):
    m, d = x.shape
    qw = q.shape[1]
    kw = kv_heads * hd
    t = Q_BLOCK
    group = qw // kw
    tm = min(rows_per_step, m)
    nb = tm // t
    assert m % tm == 0 and tm % t == 0
    prev = lambda col: pl.BlockSpec((t, kw), lambda i: (jnp.maximum(i * nb - 1, 0), col))
    return pl.pallas_call(
        functools.partial(_swa_prompt_kernel, kv_heads=kv_heads, group=group, hd=hd, scale=scale),
        grid=(m // tm,),
        in_specs=[pl.BlockSpec(memory_space=pltpu.SMEM),
                  pl.BlockSpec((tm, qw), lambda i: (i, 0)),
                  pl.BlockSpec((tm, 2 * kw), lambda i: (i, 0)),
                  prev(0), prev(1),
                  pl.BlockSpec((tm, d), lambda i: (i, 0)),
                  _const_spec(w_o.shape), _const_spec((1, d))],
        out_specs=pl.BlockSpec((tm, d), lambda i: (i, 0)),
        out_shape=jax.ShapeDtypeStruct((m, d), F32),
        scratch_shapes=[pltpu.VMEM((tm, qw), F32), pltpu.VMEM((nb * kv_heads, group * t, 2 * t), F32),
                        pltpu.VMEM((nb * kv_heads, group * t, 2 * t), BF16)],
        compiler_params=_cparams(1),
        name="swa_prompt_attn",
    )(sinks, q, kv, kv, kv, x, w_o, b_o.reshape(1, d))


def _swa_sample_kernel(sinks_ref, q_ref, new_ref, ck_ref, cv_ref, x_ref, wo_ref, bo_ref, y_ref, o_scr,
                       *, kv_heads, group, hd, dec_seq, scale):
    n_seq, wb, kw = ck_ref.shape
    t = n_seq * dec_seq
    q = q_ref[...]
    kk = jnp.concatenate([ck_ref[...].reshape(n_seq * wb, kw), new_ref[:, :kw]], axis=0).astype(BF16)
    vv = jnp.concatenate([cv_ref[...].reshape(n_seq * wb, kw), new_ref[:, kw:]], axis=0).astype(BF16)
    n_old = n_seq * wb
    shape = (group * t, n_old + t)
    r = _irem(lax.broadcasted_iota(jnp.int32, shape, 0), t)
    r_seq, r_tok = _idiv(r, dec_seq), _irem(r, dec_seq)
    j = lax.broadcasted_iota(jnp.int32, shape, 1)
    old = j < n_old
    j_new = jnp.maximum(j - n_old, 0)
    j_seq = jnp.where(old, _idiv(j, wb), _idiv(j_new, dec_seq))
    j_pos = jnp.where(old, _irem(j, wb), wb + _irem(j_new, dec_seq))
    valid = (j_seq == r_seq) & (j_pos > r_tok) & (j_pos <= r_tok + wb)
    for kvh in range(kv_heads):
        qg = _stack_group(q, kvh, group, hd).astype(BF16)
        s = _dot_nt(qg, kk[:, kvh * hd:(kvh + 1) * hd]) * scale
        o = _sink_attend(s, valid, _sink_rows(sinks_ref, kvh, group, t), vv[:, kvh * hd:(kvh + 1) * hd])
        for g in range(group):
            hh = kvh * group + g
            o_scr[:, hh * hd:(hh + 1) * hd] = o[g * t:(g + 1) * t]
    _swa_finish(o_scr, x_ref, wo_ref, bo_ref, y_ref)


def _swa_sample_attn(q, kv_new, cache_k, cache_v, x, sinks, w_o, b_o, kv_heads, hd, dec_seq, scale, n_seq=8):
    m, d = x.shape
    qw = q.shape[1]
    kw = kv_heads * hd
    b, wb = cache_k.shape[:2]
    n_seq = min(n_seq, b)
    t = n_seq * dec_seq
    group = qw // kw
    return pl.pallas_call(
        functools.partial(_swa_sample_kernel, kv_heads=kv_heads, group=group, hd=hd, dec_seq=dec_seq, scale=scale),
        grid=(b // n_seq,),
        in_specs=[pl.BlockSpec(memory_space=pltpu.SMEM),
                  pl.BlockSpec((t, qw), lambda i: (i, 0)),
                  pl.BlockSpec((t, 2 * kw), lambda i: (i, 0)),
                  pl.BlockSpec((n_seq, wb, kw), lambda i: (i, 0, 0)),
                  pl.BlockSpec((n_seq, wb, kw), lambda i: (i, 0, 0)),
                  pl.BlockSpec((t, d), lambda i: (i, 0)),
                  _const_spec(w_o.shape), _const_spec((1, d))],
        out_specs=pl.BlockSpec((t, d), lambda i: (i, 0)),
        out_shape=jax.ShapeDtypeStruct((m, d), F32),
        scratch_shapes=[pltpu.VMEM((t, qw), F32)],
        compiler_params=_cparams(1),
        name="swa_sample_attn",
    )(sinks, q, kv_new, cache_k.reshape(b, wb, kw), cache_v.reshape(b, wb, kw), x, w_o, b_o.reshape(1, d))


def _prep_weights(p):
    n_a, kv_lora, heads, nope = p["mla_w_uk"].shape
    q_lora = p["mla_w_dq"].shape[2]
    w = {"ffn_in": p["ffn_w_in"].astype(BF16), "ffn_out": p["ffn_w_out"].astype(BF16), "mla": []}
    for l in range(n_a):
        w_uq = p["mla_w_uq"][l].reshape(q_lora, heads, nope + ROPE_DIM)
        w_dkv = p["mla_w_dkv"][l]
        w["mla"].append({
            "norm": p["mla_norm"][l], "q_norm": p["mla_q_norm"][l], "kv_norm": p["mla_kv_norm"][l],
            "w_dq": p["mla_w_dq"][l].astype(BF16),
            "w_uq_n": w_uq[:, :, :nope].reshape(q_lora, heads * nope).astype(BF16),
            "w_uq_r": w_uq[:, :, nope:].reshape(q_lora, heads * ROPE_DIM).astype(BF16),
            "w_uk": jnp.transpose(p["mla_w_uk"][l], (1, 2, 0)).astype(BF16),
            "w_dkv_c": w_dkv[:, :kv_lora].astype(BF16),
            "w_dkv_r": jnp.pad(w_dkv[:, kv_lora:], ((0, 0), (0, LANES - ROPE_DIM))).astype(BF16),
            "w_uv": jnp.transpose(p["mla_w_uv"][l], (1, 0, 2)).astype(BF16),
            "w_o": p["mla_w_o"][l].astype(BF16),
        })
    w["swa_kv"] = p["swa_w_kv"].astype(BF16)
    w["swa_q"] = p["swa_w_q"].astype(BF16)
    w["swa_o"] = p["swa_w_o"].astype(BF16)
    return w


def _run_trunk(x, pos, p, w, mla_q_scale, mla_attend, swa_attend):
    depth = p["ffn_norm"].shape[0]
    n_a = p["mla_norm"].shape[0]
    kv_w = w["swa_kv"].shape[1]
    cos_t, sin_t = _rope_tables(pos)
    rows_all, kv = [], None
    for l in range(depth):
        if l == n_a:
            kv = _norm_linear(x, p["kv_norm"], w["swa_kv"], p["swa_b_kv"], cos_t, sin_t, rope_cols=kv_w // 2)
        x = _ffn(x, p["ffn_norm"][l, 0], w["ffn_in"][l, 0], w["ffn_out"][l, 0])
        if l < n_a:
            wl = w["mla"][l]
            q, rows, rows_bf = _mla_proj(x, wl, cos_t, sin_t, mla_q_scale)
            o_lat = mla_attend(l, q, rows, rows_bf)
            x = _mla_out(o_lat, x, wl["w_uv"], wl["w_o"])
            rows_all.append(rows)
        else:
            jl = l - n_a
            q = _norm_linear(x, p["swa_norm"][jl], w["swa_q"][jl], p["swa_b_q"][jl], cos_t, sin_t,
                             rope_cols=w["swa_q"].shape[2])
            x = swa_attend(q, kv, x, p["swa_sinks"][jl], w["swa_o"][jl], p["swa_b_o"][jl])
        x = _ffn(x, p["ffn_norm"][l, 1], w["ffn_in"][l, 1], w["ffn_out"][l, 1],
                 final_g=p["final_norm"] if l == depth - 1 else None)
    return x, jnp.stack(rows_all, axis=0), kv


def kernel(x_prompt, x_sample, cache_mla, cache_swa_k, cache_swa_v, page_table, ffn_norm, ffn_w_in, ffn_w_out, mla_norm, mla_w_dq, mla_q_norm, mla_w_uq, mla_w_dkv, mla_kv_norm, mla_w_uk, mla_w_uv, mla_w_o, kv_norm, swa_w_kv, swa_b_kv, swa_norm, swa_w_q, swa_b_q, swa_sinks, swa_w_o, swa_b_o, final_norm):
    p = {
        "ffn_norm": ffn_norm, "ffn_w_in": ffn_w_in, "ffn_w_out": ffn_w_out,
        "mla_norm": mla_norm, "mla_w_dq": mla_w_dq, "mla_q_norm": mla_q_norm, "mla_w_uq": mla_w_uq,
        "mla_w_dkv": mla_w_dkv, "mla_kv_norm": mla_kv_norm, "mla_w_uk": mla_w_uk, "mla_w_uv": mla_w_uv,
        "mla_w_o": mla_w_o,
        "kv_norm": kv_norm, "swa_w_kv": swa_w_kv, "swa_b_kv": swa_b_kv,
        "swa_norm": swa_norm, "swa_w_q": swa_w_q, "swa_b_q": swa_b_q, "swa_sinks": swa_sinks,
        "swa_w_o": swa_w_o, "swa_b_o": swa_b_o, "final_norm": final_norm,
    }
    batch, seq, d = x_prompt.shape
    dec_batch, dec_seq, _ = x_sample.shape
    _, kv_lora, heads, nope = mla_w_uk.shape
    e = kv_lora + ROPE_DIM
    kv_heads, hd = cache_swa_k.shape[2:]
    past_len = page_table.shape[1] * cache_mla.shape[2]
    mla_q_scale = float(nope + ROPE_DIM) ** -0.5 * LOG2_E
    swa_scale = float(hd) ** -0.5
    cache_t = jnp.swapaxes(cache_mla, 2, 3)
    assert batch == 1 and seq % Q_BLOCK == 0 and cache_swa_k.shape[1] == Q_BLOCK
    w = _prep_weights(p)

    def mla_prompt(l, q, rows, rows_bf):
        return _mla_prompt_attn(q, rows_bf, kv_lora)

    def swa_prompt(q, kv, x, sinks, w_o, b_o):
        return _swa_prompt_attn(q, kv, x, sinks, w_o, b_o, kv_heads, hd, swa_scale)

    y_p, rows_p, kv_p = _run_trunk(x_prompt.reshape(seq, d), jnp.arange(seq, dtype=jnp.int32), p, w, mla_q_scale,
                                   mla_prompt, swa_prompt)

    n_tok = dec_batch * dec_seq

    def mla_sample(l, q, rows, rows_bf):
        q_tok = jnp.transpose(q, (0, 2, 1, 3)).reshape(dec_batch, dec_seq, heads, e)
        q_seq = jnp.transpose(q_tok, (0, 2, 1, 3)).reshape(dec_batch, heads * dec_seq, e)
        o = _mla_sample_attn(q_seq, rows.reshape(dec_batch, dec_seq, e), cache_t, l, page_table, kv_lora)
        o = jnp.transpose(o.reshape(dec_batch, heads, dec_seq, kv_lora), (1, 0, 2, 3))
        return o.reshape(1, heads, n_tok, kv_lora)

    def swa_sample(q, kv, x, sinks, w_o, b_o):
        return _swa_sample_attn(q, kv, cache_swa_k, cache_swa_v, x, sinks, w_o, b_o, kv_heads, hd, dec_seq, swa_scale)

    pos_s = jnp.tile(past_len + jnp.arange(dec_seq, dtype=jnp.int32), dec_batch)
    y_s, rows_s, kv_s = _run_trunk(x_sample.reshape(n_tok, d), pos_s, p, w, mla_q_scale, mla_sample, swa_sample)

    kw = kv_heads * hd
    w_p = min(Q_BLOCK, seq)
    k_p = kv_p[seq - w_p:, :kw].reshape(1, w_p, kv_heads, hd)
    v_p = kv_p[seq - w_p:, kw:].reshape(1, w_p, kv_heads, hd)
    k_s = kv_s[:, :kw].reshape(dec_batch, dec_seq, kv_heads, hd)
    v_s = kv_s[:, kw:].reshape(dec_batch, dec_seq, kv_heads, hd)
    swa_k_sample = jnp.concatenate([cache_swa_k, k_s], axis=1)[:, dec_seq:]
    swa_v_sample = jnp.concatenate([cache_swa_v, v_s], axis=1)[:, dec_seq:]
    return (y_p.reshape(1, seq, d), y_s.reshape(dec_batch, dec_seq, d),
            rows_p.reshape(-1, 1, seq, e), rows_s.reshape(-1, dec_batch, dec_seq, e),
            k_p, v_p, swa_k_sample, swa_v_sample)
```

```python
import functools

import jax
import jax.numpy as jnp
from jax import lax
from jax.experimental import pallas as pl
from jax.experimental.pallas import tpu as pltpu

F32 = jnp.float32
BF16 = jnp.bfloat16

EPS = 1e-6
ROPE_THETA = 10000.0
FFN_RES_WEIGHT = 0.5
LANES = 128
MXU_DIM = 256
ROPE_DIM = 64
Q_BLOCK = 128
VMEM_LIMIT = 56 * 1024 * 1024
NEG_INF = float("-inf")
LOG2_E = 1.4426950408889634


def _cparams(n_axes, vmem=None):
    return pltpu.CompilerParams(dimension_semantics=("arbitrary",) * n_axes, vmem_limit_bytes=vmem)


def _const_spec(shape):
    nd = len(shape)
    return pl.BlockSpec(shape, lambda *_: (0,) * nd, pipeline_mode=pl.Buffered(1))


def _dot(a, b):
    return jnp.dot(a, b, preferred_element_type=F32)


def _dot_nt(a, b):
    return lax.dot_general(a, b, (((1,), (1,)), ((), ())), preferred_element_type=F32)


def _idiv(x, n):
    if n & (n - 1) == 0:
        return x >> (n.bit_length() - 1)
    return lax.div(x, jnp.full(x.shape, n, x.dtype))


def _irem(x, n):
    if n & (n - 1) == 0:
        return x & (n - 1)
    return lax.rem(x, jnp.full(x.shape, n, x.dtype))


def _rms(x, g):
    return x * lax.rsqrt(jnp.mean(x * x, axis=-1, keepdims=True) + EPS) * g


def _rope128(x, cos_t, sin_t):
    lane = lax.broadcasted_iota(jnp.int32, x.shape, 1)
    first_half = (lane & (ROPE_DIM // 2)) == 0
    partner = jnp.where(first_half, pltpu.roll(x, LANES - ROPE_DIM // 2, 1), pltpu.roll(x, ROPE_DIM // 2, 1))
    return x * cos_t + partner * sin_t


def _rope_tables(pos):
    half = ROPE_DIM // 2
    inv_freq = jnp.exp(-(jnp.arange(half, dtype=F32) / half) * jnp.log(F32(ROPE_THETA)))
    ang = pos.astype(F32)[:, None] * inv_freq[None, :]
    cos, sin = jnp.cos(ang), jnp.sin(ang)
    reps = LANES // ROPE_DIM
    cos_t = jnp.tile(cos, (1, 2 * reps))
    sin_t = jnp.tile(jnp.concatenate([-sin, sin], axis=1), (1, reps))
    return cos_t, sin_t


def _norm_linear_rope(x, g_ref, w_ref, b_ref, cos_ref, sin_ref, z_ref, rope_cols):
    z = _dot(_rms(x, g_ref[...]).astype(BF16), w_ref[...]) + b_ref[...]
    cos_t, sin_t = cos_ref[...], sin_ref[...]
    for c in range(rope_cols // LANES):
        z_ref[:, c * LANES:(c + 1) * LANES] = _rope128(z[:, c * LANES:(c + 1) * LANES], cos_t, sin_t)
    if rope_cols < z.shape[1]:
        z_ref[:, rope_cols:] = z[:, rope_cols:]


def _ffn_kernel(x_ref, g_ref, win_ref, wout_ref, *rest, d_ff, bounds, final, rope_cols):
    rest = list(rest)
    gf_ref = rest.pop(0) if final else None
    proj = [rest.pop(0) for _ in range(5)] if rope_cols is not None else None
    o_ref = rest.pop(0)
    x = x_ref[...]
    h = _rms(x, g_ref[...]).astype(BF16)
    acc = None
    for c0, c1 in zip(bounds[:-1], bounds[1:]):
        gate = _dot(h, win_ref[:, c0:c1])
        up = _dot(h, win_ref[:, d_ff + c0:d_ff + c1])
        act = (gate * (1.0 / (1.0 + jnp.exp(-gate))) * up).astype(BF16)
        part = _dot(act, wout_ref[c0:c1, :])
        acc = part if acc is None else acc + part
    y = x + FFN_RES_WEIGHT * acc
    if proj is not None:
        _norm_linear_rope(y, *proj, rest.pop(0), rope_cols)
    if final:
        y = _rms(y, gf_ref[...])
    o_ref[...] = y


def _ffn(x, g, w_in, w_out, half, final_g=None, proj=None, tm=512):
    m, d = x.shape
    d_ff = w_out.shape[2]
    pick = lambda r, c: pl.BlockSpec((None, None, r, c), lambda i: (*half, 0, 0), pipeline_mode=pl.Buffered(1))
    tm = min(tm, m)
    unit = MXU_DIM if d_ff % MXU_DIM == 0 else LANES
    n_units = d_ff // unit
    bounds = (0, (n_units + 1) // 2 * unit, d_ff) if n_units > 1 and d_ff % unit == 0 else (0, d_ff)
    in_specs = [
        pl.BlockSpec((tm, d), lambda i: (i, 0)),
        _const_spec((1, d)),
        pick(d, 2 * d_ff),
        pick(d_ff, d),
    ]
    args = [x, g.reshape(1, d), w_in, w_out]
    if final_g is not None:
        in_specs.append(_const_spec((1, d)))
        args.append(final_g.reshape(1, d))
    row_spec = lambda width: pl.BlockSpec((tm, width), lambda i: (i, 0))
    out_specs, out_shape, rope_cols = [row_spec(d)], [jax.ShapeDtypeStruct((m, d), F32)], None
    if proj is not None:
        g2, w2, b2, cos_t, sin_t, rope_cols = proj
        n = w2.shape[1]
        in_specs += [_const_spec((1, d)), _const_spec((d, n)), _const_spec((1, n)), row_spec(LANES), row_spec(LANES)]
        args += [g2.reshape(1, d), w2, b2.reshape(1, n), cos_t, sin_t]
        out_specs.append(row_spec(n))
        out_shape.append(jax.ShapeDtypeStruct((m, n), F32))
    out = pl.pallas_call(
        functools.partial(_ffn_kernel, d_ff=d_ff, bounds=bounds, final=final_g is not None, rope_cols=rope_cols),
        grid=(m // tm,),
        in_specs=in_specs,
        out_specs=out_specs,
        out_shape=out_shape,
        compiler_params=_cparams(1, VMEM_LIMIT),
        name="ffn_half",
    )(*args)
    return out if proj is not None else out[0]


def _mla_proj_kernel(x_ref, g_ref, wdq_ref, gq_ref, wuqn_ref, wuqr_ref, wuk_ref, wdkvc_ref, wdkvr_ref,
                     gkv_ref, cos_ref, sin_ref, q_ref, rows_ref, rowsb_ref, *, heads, nope, kv_lora, q_scale):
    tm = x_ref.shape[0]
    cos_t, sin_t = cos_ref[...], sin_ref[...]
    h = _rms(x_ref[...], g_ref[...]).astype(BF16)
    cq = _rms(_dot(h, wdq_ref[...]), gq_ref[...]).astype(BF16)
    qn = _dot(cq, wuqn_ref[...])
    qr = _dot(cq, wuqr_ref[...])
    qr = jnp.concatenate(
        [_rope128(qr[:, c * LANES:(c + 1) * LANES], cos_t, sin_t) for c in range(qr.shape[1] // LANES)], axis=1)
    c_lat = _rms(_dot(h, wdkvc_ref[...]), gkv_ref[...])
    k_pe = _rope128(_dot(h, wdkvr_ref[...]), cos_t, sin_t)[:, :ROPE_DIM]
    rows_ref[:, :kv_lora] = c_lat
    rows_ref[:, kv_lora:] = k_pe
    rowsb_ref[:, :kv_lora] = c_lat.astype(BF16)
    rowsb_ref[:, kv_lora:] = k_pe.astype(BF16)
    for hd in range(heads):
        q_lat = (_dot(qn[:, hd * nope:(hd + 1) * nope].astype(BF16), wuk_ref[hd]) * q_scale).astype(BF16)
        q_pe = (qr[:, hd * ROPE_DIM:(hd + 1) * ROPE_DIM] * q_scale).astype(BF16)
        for r in range(tm // Q_BLOCK):
            q_ref[r, hd, :, :kv_lora] = q_lat[r * Q_BLOCK:(r + 1) * Q_BLOCK]
            q_ref[r, hd, :, kv_lora:] = q_pe[r * Q_BLOCK:(r + 1) * Q_BLOCK]


def _mla_proj(x, w, cos_t, sin_t, q_scale, tm=512):
    m, d = x.shape
    heads, nope, kv_lora = w["w_uk"].shape
    e = kv_lora + ROPE_DIM
    q_lora = w["w_dq"].shape[1]
    tm = min(tm, m)
    nb = tm // Q_BLOCK
    consts = [w["norm"].reshape(1, d), w["w_dq"], w["q_norm"].reshape(1, q_lora), w["w_uq_n"], w["w_uq_r"],
              w["w_uk"], w["w_dkv_c"], w["w_dkv_r"], w["kv_norm"].reshape(1, kv_lora)]
    row_spec = lambda width: pl.BlockSpec((tm, width), lambda i: (i, 0))
    return pl.pallas_call(
        functools.partial(_mla_proj_kernel, heads=heads, nope=nope, kv_lora=kv_lora, q_scale=q_scale),
        grid=(m // tm,),
        in_specs=[row_spec(d)] + [_const_spec(c.shape) for c in consts] + [row_spec(LANES), row_spec(LANES)],
        out_specs=[pl.BlockSpec((nb, heads, Q_BLOCK, e), lambda i: (i, 0, 0, 0)), row_spec(e), row_spec(e)],
        out_shape=[jax.ShapeDtypeStruct((m // Q_BLOCK, heads, Q_BLOCK, e), BF16),
                   jax.ShapeDtypeStruct((m, e), F32),
                   jax.ShapeDtypeStruct((m, e), BF16)],
        compiler_params=_cparams(1),
        name="mla_proj",
    )(x, *consts, cos_t, sin_t)


def _flash_scores(q_ref, k_ref, s_scr, slot, start, *, tk):
    heads, _, e = q_ref.shape[1:]
    s_scr[slot] = _dot_nt(q_ref[0].reshape(heads * Q_BLOCK, e), k_ref[pl.ds(start, tk), :])


def _flash_update(k_ref, s_scr, slot, start, q_pos0, m_ref, l_ref, acc_ref, *, tk, kv_lora):
    s = s_scr[slot]
    if q_pos0 is not None:
        q_pos = q_pos0 + (lax.broadcasted_iota(jnp.int32, s.shape, 0) & (Q_BLOCK - 1))
        k_pos = start + lax.broadcasted_iota(jnp.int32, s.shape, 1)
        s = jnp.where(k_pos <= q_pos, s, NEG_INF)
    chunks = [s[:, j * LANES:(j + 1) * LANES] for j in range(tk // LANES)]
    m_prev = m_ref[...]
    m_new = jnp.maximum(m_prev, jnp.max(functools.reduce(jnp.maximum, chunks), axis=-1, keepdims=True))
    alpha = jnp.exp2(m_prev - m_new)
    ps = [jnp.exp2(ch - m_new) for ch in chunks]
    l_ref[...] = alpha * l_ref[...] + functools.reduce(jnp.add, ps)
    pv = _dot(jnp.concatenate(ps, axis=1).astype(BF16), k_ref[pl.ds(start, tk), :kv_lora])
    acc_ref[...] = acc_ref[...] * jnp.concatenate([alpha] * (kv_lora // LANES), axis=1) + pv
    m_ref[...] = m_new


def _mla_prompt_kernel(q_ref, k_ref, o_ref, s_scr, m_ref, l_ref, acc_ref, *, tk, kv_lora):
    i = pl.program_id(0)
    heads = q_ref.shape[1]
    m_ref[...] = jnp.full(m_ref.shape, NEG_INF, F32)
    l_ref[...] = jnp.zeros(l_ref.shape, F32)
    acc_ref[...] = jnp.zeros(acc_ref.shape, F32)
    scores = functools.partial(_flash_scores, q_ref, k_ref, s_scr, tk=tk)
    update = functools.partial(_flash_update, k_ref, s_scr, m_ref=m_ref, l_ref=l_ref, acc_ref=acc_ref,
                               tk=tk, kv_lora=kv_lora)
    at = lambda j: pl.multiple_of(j * tk, tk)
    n_full = (i * Q_BLOCK) // tk
    n_pairs = n_full // 2
    scores(0, at(0))

    def body(p, carry):
        j = 2 * p
        scores(1, at(j + 1))
        update(0, at(j), None)
        scores(0, at(j + 2))
        update(1, at(j + 1), None)
        return carry

    lax.fori_loop(0, n_pairs, body, 0)
    j0 = 2 * n_pairs
    q_pos0 = i * Q_BLOCK

    @pl.when(n_full == j0)
    def _():
        update(0, at(j0), q_pos0)

    @pl.when(n_full != j0)
    def _():
        scores(1, at(j0 + 1))
        update(0, at(j0), None)
        update(1, at(j0 + 1), q_pos0)

    l_tot = jnp.sum(l_ref[...], axis=-1, keepdims=True)
    o_ref[0] = (acc_ref[...] / l_tot).astype(o_ref.dtype).reshape(heads, Q_BLOCK, kv_lora)


def _mla_prompt_attn(q, rows_bf, kv_lora, tk=1024):
    nb, heads, _, e = q.shape
    s = rows_bf.shape[0]
    tk = min(tk, s)
    assert s % tk == 0 and tk % Q_BLOCK == 0
    rows = heads * Q_BLOCK
    return pl.pallas_call(
        functools.partial(_mla_prompt_kernel, tk=tk, kv_lora=kv_lora),
        grid=(nb,),
        in_specs=[pl.BlockSpec((1, heads, Q_BLOCK, e), lambda i: (i, 0, 0, 0)), _const_spec((s, e))],
        out_specs=pl.BlockSpec((1, heads, Q_BLOCK, kv_lora), lambda i: (i, 0, 0, 0)),
        out_shape=jax.ShapeDtypeStruct((nb, heads, Q_BLOCK, kv_lora), BF16),
        scratch_shapes=[pltpu.VMEM((2, rows, tk), F32), pltpu.VMEM((rows, LANES), F32),
                        pltpu.VMEM((rows, LANES), F32), pltpu.VMEM((rows, kv_lora), F32)],
        compiler_params=_cparams(1, VMEM_LIMIT),
        name="mla_prompt_attn",
    )(q, rows_bf)


def _mla_sample_kernel(pt_ref, q_ref, new_ref, cache_ref, o_ref, pbuf, sem, kbuf_ref, nbuf_ref, m_ref, l_ref, acc_ref,
                       *, layer, n_group, dec_seq, kv_lora):
    p_idx = pl.program_id(1)
    n_steps = pl.num_programs(0) * pl.num_programs(1)
    t = pl.program_id(0) * pl.num_programs(1) + p_idx
    n_slots = pbuf.shape[0]
    ahead = n_slots - 1
    slot = lax.rem(t, n_slots)
    page = pbuf.shape[3]

    def page_copies(step, sl):
        return [pltpu.make_async_copy(cache_ref.at[layer, pt_ref[step * n_group + g]], pbuf.at[sl, g], sem.at[sl])
                for g in range(n_group)]

    @pl.when(t == 0)
    def _():
        for step in range(ahead):
            @pl.when(step < n_steps)
            def _():
                for c in page_copies(step, step):
                    c.start()

    @pl.when(t + ahead < n_steps)
    def _():
        for c in page_copies(t + ahead, lax.rem(t + ahead, n_slots)):
            c.start()

    q = q_ref[...]

    @pl.when(p_idx == 0)
    def _():
        nbuf_ref[...] = jnp.zeros(nbuf_ref.shape, F32)
        nbuf_ref[:dec_seq, :] = new_ref[...]
        nk = nbuf_ref[...].astype(BF16)
        s = _dot_nt(q, nk)
        tok = _irem(lax.broadcasted_iota(jnp.int32, s.shape, 0), dec_seq)
        s = jnp.where(lax.broadcasted_iota(jnp.int32, s.shape, 1) <= tok, s, NEG_INF)
        m0 = jnp.max(s, axis=-1, keepdims=True)
        p0 = jnp.exp2(s - m0)
        m_ref[...] = m0
        l_ref[...] = jnp.sum(p0, axis=-1, keepdims=True)
        acc_ref[...] = _dot(p0.astype(BF16), nk[:, :kv_lora])

    for c in page_copies(t, slot):
        c.wait()
    for g in range(n_group):
        kbuf_ref[:, g * page:(g + 1) * page] = pbuf[slot, g].astype(BF16)
    kt = kbuf_ref[...]
    s = _dot(q, kt)
    m_prev = m_ref[...]
    m_new = jnp.maximum(m_prev, jnp.max(s, axis=-1, keepdims=True))
    alpha = jnp.exp2(m_prev - m_new)
    p = jnp.exp2(s - m_new)
    l_ref[...] = alpha * l_ref[...] + jnp.sum(p, axis=-1, keepdims=True)
    acc_ref[...] = alpha * acc_ref[...] + _dot_nt(p.astype(BF16), kt[:kv_lora, :])
    m_ref[...] = m_new

    @pl.when(p_idx == pl.num_programs(1) - 1)
    def _():
        o_ref[...] = (acc_ref[...] / l_ref[...]).astype(o_ref.dtype)


def _mla_sample_attn(q, new_rows, cache_t, layer, page_table, kv_lora, n_group=64, n_slots=3):
    b, qrows, e = q.shape
    dec_seq = new_rows.shape[1]
    n_pages = page_table.shape[1]
    page = cache_t.shape[3]
    n_group = min(n_group, n_pages)
    assert n_pages % n_group == 0 and dec_seq <= page

    grid_spec = pltpu.PrefetchScalarGridSpec(
        num_scalar_prefetch=1,
        grid=(b, n_pages // n_group),
        in_specs=[pl.BlockSpec((None, qrows, e), lambda bi, pi, pt: (bi, 0, 0)),
                  pl.BlockSpec((None, dec_seq, e), lambda bi, pi, pt: (bi, 0, 0)),
                  pl.BlockSpec(memory_space=pl.ANY)],
        out_specs=pl.BlockSpec((None, qrows, kv_lora), lambda bi, pi, pt: (bi, 0, 0)),
        scratch_shapes=[pltpu.VMEM((n_slots, n_group, e, page), F32), pltpu.SemaphoreType.DMA((n_slots,)),
                        pltpu.VMEM((e, n_group * page), BF16), pltpu.VMEM((page, e), F32),
                        pltpu.VMEM((qrows, 1), F32), pltpu.VMEM((qrows, 1), F32), pltpu.VMEM((qrows, kv_lora), F32)],
    )
    return pl.pallas_call(
        functools.partial(_mla_sample_kernel, layer=layer, n_group=n_group, dec_seq=dec_seq, kv_lora=kv_lora),
        grid_spec=grid_spec,
        out_shape=jax.ShapeDtypeStruct((b, qrows, kv_lora), BF16),
        compiler_params=_cparams(2),
        name="mla_sample_attn",
    )(page_table.reshape(-1), q, new_rows, cache_t)


def _mla_out_kernel(o_ref, x_ref, wuv_ref, wo_ref, y_ref, ocat_scr):
    nblk, heads, tq, kv_lora = o_ref.shape
    v_dim = wuv_ref.shape[2]
    for hd in range(heads):
        o_h = _dot(o_ref[:, hd].reshape(nblk * tq, kv_lora), wuv_ref[hd])
        ocat_scr[:, hd * v_dim:(hd + 1) * v_dim] = o_h.astype(BF16)
    y_ref[...] = x_ref[...] + _dot(ocat_scr[...], wo_ref[...])


def _mla_out(o_lat, x, w_uv, w_o, rows_per_step=512):
    nb, heads, tq, kv_lora = o_lat.shape
    d = x.shape[1]
    assert o_lat.dtype == BF16
    nblk = max(1, min(nb, rows_per_step // tq))
    assert nb % nblk == 0
    return pl.pallas_call(
        _mla_out_kernel,
        grid=(nb // nblk,),
        in_specs=[pl.BlockSpec((nblk, heads, tq, kv_lora), lambda i: (i, 0, 0, 0)),
                  pl.BlockSpec((nblk * tq, d), lambda i: (i, 0)),
                  _const_spec(w_uv.shape), _const_spec(w_o.shape)],
        out_specs=pl.BlockSpec((nblk * tq, d), lambda i: (i, 0)),
        out_shape=jax.ShapeDtypeStruct(x.shape, F32),
        scratch_shapes=[pltpu.VMEM((nblk * tq, w_o.shape[0]), BF16)],
        compiler_params=_cparams(1),
        name="mla_out",
    )(o_lat, x, w_uv, w_o)


def _norm_linear_kernel(x_ref, g_ref, w_ref, b_ref, cos_ref, sin_ref, y_ref, *, rope_cols):
    _norm_linear_rope(x_ref[...], g_ref, w_ref, b_ref, cos_ref, sin_ref, y_ref, rope_cols)


def _norm_linear(x, g, w, b, cos_t, sin_t, rope_cols, tm=512):
    m, d = x.shape
    n = w.shape[1]
    tm = min(tm, m)
    row_spec = lambda width: pl.BlockSpec((tm, width), lambda i: (i, 0))
    return pl.pallas_call(
        functools.partial(_norm_linear_kernel, rope_cols=rope_cols),
        grid=(m // tm,),
        in_specs=[row_spec(d), _const_spec((1, d)), _const_spec((d, n)), _const_spec((1, n)),
                  row_spec(LANES), row_spec(LANES)],
        out_specs=row_spec(n),
        out_shape=jax.ShapeDtypeStruct((m, n), F32),
        compiler_params=_cparams(1),
        name="norm_linear_rope",
    )(x, g.reshape(1, d), w, b.reshape(1, n), cos_t, sin_t)


def _sink_attend(s, valid, sink, v):
    s = jnp.where(valid, s, NEG_INF)
    m = jnp.maximum(jnp.max(s, axis=-1, keepdims=True), sink)
    p = jnp.exp(s - m)
    pr = p / (jnp.sum(p, axis=-1, keepdims=True) + jnp.exp(sink - m))
    return _dot(pr.astype(BF16), v)


def _stack_group(q, kvh, group, hd):
    return jnp.concatenate([q[:, (kvh * group + g) * hd:(kvh * group + g + 1) * hd] for g in range(group)], axis=0)


def _sink_rows(sinks_ref, kvh, group, t):
    row = lax.broadcasted_iota(jnp.int32, (group * t, 1), 0)
    sink = jnp.full((group * t, 1), sinks_ref[kvh * group], F32)
    for g in range(1, group):
        sink = jnp.where(row >= g * t, sinks_ref[kvh * group + g], sink)
    return sink


def _swa_finish(o_scr, x_ref, wo_ref, bo_ref, y_ref):
    y_ref[...] = x_ref[...] + _dot(o_scr[...].astype(BF16), wo_ref[...]) + bo_ref[...]


def _swa_prompt_kernel(sinks_ref, q_ref, cur_ref, pk_ref, pv_ref, x_ref, wo_ref, bo_ref, y_ref, o_scr, s_scr, p_scr,
                       *, kv_heads, group, hd, scale):
    i = pl.program_id(0)
    t = Q_BLOCK
    nb = q_ref.shape[0] // t
    kw = kv_heads * hd
    shape = (group * t, 2 * t)
    r = _irem(lax.broadcasted_iota(jnp.int32, shape, 0), t)
    j = lax.broadcasted_iota(jnp.int32, shape, 1)
    band = (j > r) & (j <= r + t)
    band_first = band & ((j >= t) | (i > 0))
    kks, vvs = [], []
    for b in range(nb):
        r0, r1 = b * t, (b + 1) * t
        pk = pk_ref[...] if b == 0 else cur_ref[r0 - t:r0, :kw]
        pv = pv_ref[...] if b == 0 else cur_ref[r0 - t:r0, kw:]
        kks.append(jnp.concatenate([pk, cur_ref[r0:r1, :kw]], axis=0).astype(BF16))
        vvs.append(jnp.concatenate([pv, cur_ref[r0:r1, kw:]], axis=0).astype(BF16))
    for b in range(nb):
        q = q_ref[b * t:(b + 1) * t, :]
        for kvh in range(kv_heads):
            qg = _stack_group(q, kvh, group, hd).astype(BF16)
            s_scr[b * kv_heads + kvh] = _dot_nt(qg, kks[b][:, kvh * hd:(kvh + 1) * hd])
    for b in range(nb):
        for kvh in range(kv_heads):
            s = jnp.where(band_first if b == 0 else band, s_scr[b * kv_heads + kvh] * scale, NEG_INF)
            sink = _sink_rows(sinks_ref, kvh, group, t)
            m = jnp.maximum(jnp.max(s, axis=-1, keepdims=True), sink)
            p = jnp.exp(s - m)
            pr = p / (jnp.sum(p, axis=-1, keepdims=True) + jnp.exp(sink - m))
            p_scr[b * kv_heads + kvh] = pr.astype(BF16)
    for b in range(nb):
        for kvh in range(kv_heads):
            o = _dot(p_scr[b * kv_heads + kvh], vvs[b][:, kvh * hd:(kvh + 1) * hd])
            for g in range(group):
                hh = kvh * group + g
                o_scr[b * t:(b + 1) * t, hh * hd:(hh + 1) * hd] = o[g * t:(g + 1) * t]
    _swa_finish(o_scr, x_ref, wo_ref, bo_ref, y_ref)


def _swa_prompt_attn(q, kv, x, sinks, w_o, b_o, kv_heads, hd, scale, rows_per_step=512):
    m, d = x.shape
    qw = q.shape[1]
    kw = kv_heads * hd
    t = Q_BLOCK
    group = qw // kw
    tm = min(rows_per_step, m)
    nb = tm // t
    assert m % tm == 0 and tm % t == 0
    prev = lambda col: pl.BlockSpec((t, kw), lambda i: (jnp.maximum(i * nb - 1, 0), col))
    return pl.pallas_call(
        functools.partial(_swa_prompt_kernel, kv_heads=kv_heads, group=group, hd=hd, scale=scale),
        grid=(m // tm,),
        in_specs=[pl.BlockSpec(memory_space=pltpu.SMEM),
                  pl.BlockSpec((tm, qw), lambda i: (i, 0)),
                  pl.BlockSpec((tm, 2 * kw), lambda i: (i, 0)),
                  prev(0), prev(1),
                  pl.BlockSpec((tm, d), lambda i: (i, 0)),
                  _const_spec(w_o.shape), _const_spec((1, d))],
        out_specs=pl.BlockSpec((tm, d), lambda i: (i, 0)),
        out_shape=jax.ShapeDtypeStruct((m, d), F32),
        scratch_shapes=[pltpu.VMEM((tm, qw), F32), pltpu.VMEM((nb * kv_heads, group * t, 2 * t), F32),
                        pltpu.VMEM((nb * kv_heads, group * t, 2 * t), BF16)],
        compiler_params=_cparams(1),
        name="swa_prompt_attn",
    )(sinks, q, kv, kv, kv, x, w_o, b_o.reshape(1, d))


def _swa_sample_kernel(sinks_ref, q_ref, new_ref, ck_ref, cv_ref, x_ref, wo_ref, bo_ref, y_ref, o_scr,
                       *, kv_heads, group, hd, dec_seq, scale):
    n_seq, wb, kw = ck_ref.shape
    t = n_seq * dec_seq
    q = q_ref[...]
    kk = jnp.concatenate([ck_ref[...].reshape(n_seq * wb, kw), new_ref[:, :kw]], axis=0).astype(BF16)
    vv = jnp.concatenate([cv_ref[...].reshape(n_seq * wb, kw), new_ref[:, kw:]], axis=0).astype(BF16)
    n_old = n_seq * wb
    shape = (group * t, n_old + t)
    r = _irem(lax.broadcasted_iota(jnp.int32, shape, 0), t)
    r_seq, r_tok = _idiv(r, dec_seq), _irem(r, dec_seq)
    j = lax.broadcasted_iota(jnp.int32, shape, 1)
    old = j < n_old
    j_new = jnp.maximum(j - n_old, 0)
    j_seq = jnp.where(old, _idiv(j, wb), _idiv(j_new, dec_seq))
    j_pos = jnp.where(old, _irem(j, wb), wb + _irem(j_new, dec_seq))
    valid = (j_seq == r_seq) & (j_pos > r_tok) & (j_pos <= r_tok + wb)
    for kvh in range(kv_heads):
        qg = _stack_group(q, kvh, group, hd).astype(BF16)
        s = _dot_nt(qg, kk[:, kvh * hd:(kvh + 1) * hd]) * scale
        o = _sink_attend(s, valid, _sink_rows(sinks_ref, kvh, group, t), vv[:, kvh * hd:(kvh + 1) * hd])
        for g in range(group):
            hh = kvh * group + g
            o_scr[:, hh * hd:(hh + 1) * hd] = o[g * t:(g + 1) * t]
    _swa_finish(o_scr, x_ref, wo_ref, bo_ref, y_ref)


def _swa_sample_attn(q, kv_new, cache_k, cache_v, x, sinks, w_o, b_o, kv_heads, hd, dec_seq, scale, n_seq=8):
    m, d = x.shape
    qw = q.shape[1]
    kw = kv_heads * hd
    b, wb = cache_k.shape[:2]
    n_seq = min(n_seq, b)
    t = n_seq * dec_seq
    group = qw // kw
    return pl.pallas_call(
        functools.partial(_swa_sample_kernel, kv_heads=kv_heads, group=group, hd=hd, dec_seq=dec_seq, scale=scale),
        grid=(b // n_seq,),
        in_specs=[pl.BlockSpec(memory_space=pltpu.SMEM),
                  pl.BlockSpec((t, qw), lambda i: (i, 0)),
                  pl.BlockSpec((t, 2 * kw), lambda i: (i, 0)),
                  pl.BlockSpec((n_seq, wb, kw), lambda i: (i, 0, 0)),
                  pl.BlockSpec((n_seq, wb, kw), lambda i: (i, 0, 0)),
                  pl.BlockSpec((t, d), lambda i: (i, 0)),
                  _const_spec(w_o.shape), _const_spec((1, d))],
        out_specs=pl.BlockSpec((t, d), lambda i: (i, 0)),
        out_shape=jax.ShapeDtypeStruct((m, d), F32),
        scratch_shapes=[pltpu.VMEM((t, qw), F32)],
        compiler_params=_cparams(1),
        name="swa_sample_attn",
    )(sinks, q, kv_new, cache_k.reshape(b, wb, kw), cache_v.reshape(b, wb, kw), x, w_o, b_o.reshape(1, d))


def _prep_weights(p):
    n_a, kv_lora, heads, nope = p["mla_w_uk"].shape
    q_lora = p["mla_w_dq"].shape[2]
    w = {"ffn_in": p["ffn_w_in"].astype(BF16), "ffn_out": p["ffn_w_out"].astype(BF16), "mla": []}
    for l in range(n_a):
        w_uq = p["mla_w_uq"][l].reshape(q_lora, heads, nope + ROPE_DIM)
        w_dkv = p["mla_w_dkv"][l]
        w["mla"].append({
            "norm": p["mla_norm"][l], "q_norm": p["mla_q_norm"][l], "kv_norm": p["mla_kv_norm"][l],
            "w_dq": p["mla_w_dq"][l].astype(BF16),
            "w_uq_n": w_uq[:, :, :nope].reshape(q_lora, heads * nope).astype(BF16),
            "w_uq_r": w_uq[:, :, nope:].reshape(q_lora, heads * ROPE_DIM).astype(BF16),
            "w_uk": jnp.transpose(p["mla_w_uk"][l], (1, 2, 0)).astype(BF16),
            "w_dkv_c": w_dkv[:, :kv_lora].astype(BF16),
            "w_dkv_r": jnp.pad(w_dkv[:, kv_lora:], ((0, 0), (0, LANES - ROPE_DIM))).astype(BF16),
            "w_uv": jnp.transpose(p["mla_w_uv"][l], (1, 0, 2)).astype(BF16),
            "w_o": p["mla_w_o"][l].astype(BF16),
        })
    w["swa_kv"] = p["swa_w_kv"].astype(BF16)
    w["swa_q"] = p["swa_w_q"].astype(BF16)
    w["swa_o"] = p["swa_w_o"].astype(BF16)
    return w


def _run_trunk(x, pos, p, w, mla_q_scale, mla_attend, swa_attend):
    depth = p["ffn_norm"].shape[0]
    n_a = p["mla_norm"].shape[0]
    kv_w = w["swa_kv"].shape[1]
    cos_t, sin_t = _rope_tables(pos)
    rows_all, kv = [], None
    kv_proj = (p["kv_norm"], w["swa_kv"], p["swa_b_kv"], cos_t, sin_t, kv_w // 2)
    for l in range(depth):
        if l == n_a and kv is None:
            kv = _norm_linear(x, *kv_proj[:5], rope_cols=kv_proj[5])
        if l < n_a:
            x = _ffn(x, p["ffn_norm"][l, 0], w["ffn_in"], w["ffn_out"], (l, 0))
            wl = w["mla"][l]
            q, rows, rows_bf = _mla_proj(x, wl, cos_t, sin_t, mla_q_scale)
            o_lat = mla_attend(l, q, rows, rows_bf)
            x = _mla_out(o_lat, x, wl["w_uv"], wl["w_o"])
            rows_all.append(rows)
        else:
            jl = l - n_a
            q_proj = (p["swa_norm"][jl], w["swa_q"][jl], p["swa_b_q"][jl], cos_t, sin_t, w["swa_q"].shape[2])
            x, q = _ffn(x, p["ffn_norm"][l, 0], w["ffn_in"], w["ffn_out"], (l, 0), proj=q_proj)
            x = swa_attend(q, kv, x, p["swa_sinks"][jl], w["swa_o"][jl], p["swa_b_o"][jl])
        last = l == depth - 1
        out = _ffn(x, p["ffn_norm"][l, 1], w["ffn_in"], w["ffn_out"], (l, 1),
                   final_g=p["final_norm"] if last else None, proj=kv_proj if l + 1 == n_a and not last else None)
        x, kv = out if isinstance(out, (list, tuple)) else (out, kv)
    return x, jnp.stack(rows_all, axis=0), kv


def kernel(x_prompt, x_sample, cache_mla, cache_swa_k, cache_swa_v, page_table, ffn_norm, ffn_w_in, ffn_w_out, mla_norm, mla_w_dq, mla_q_norm, mla_w_uq, mla_w_dkv, mla_kv_norm, mla_w_uk, mla_w_uv, mla_w_o, kv_norm, swa_w_kv, swa_b_kv, swa_norm, swa_w_q, swa_b_q, swa_sinks, swa_w_o, swa_b_o, final_norm):
    p = {
        "ffn_norm": ffn_norm, "ffn_w_in": ffn_w_in, "ffn_w_out": ffn_w_out,
        "mla_norm": mla_norm, "mla_w_dq": mla_w_dq, "mla_q_norm": mla_q_norm, "mla_w_uq": mla_w_uq,
        "mla_w_dkv": mla_w_dkv, "mla_kv_norm": mla_kv_norm, "mla_w_uk": mla_w_uk, "mla_w_uv": mla_w_uv,
        "mla_w_o": mla_w_o,
        "kv_norm": kv_norm, "swa_w_kv": swa_w_kv, "swa_b_kv": swa_b_kv,
        "swa_norm": swa_norm, "swa_w_q": swa_w_q, "swa_b_q": swa_b_q, "swa_sinks": swa_sinks,
        "swa_w_o": swa_w_o, "swa_b_o": swa_b_o, "final_norm": final_norm,
    }
    batch, seq, d = x_prompt.shape
    dec_batch, dec_seq, _ = x_sample.shape
    _, kv_lora, heads, nope = mla_w_uk.shape
    e = kv_lora + ROPE_DIM
    kv_heads, hd = cache_swa_k.shape[2:]
    past_len = page_table.shape[1] * cache_mla.shape[2]
    mla_q_scale = float(nope + ROPE_DIM) ** -0.5 * LOG2_E
    swa_scale = float(hd) ** -0.5
    cache_t = jnp.swapaxes(cache_mla, 2, 3)
    assert batch == 1 and seq % Q_BLOCK == 0 and cache_swa_k.shape[1] == Q_BLOCK
    w = _prep_weights(p)

    def mla_prompt(l, q, rows, rows_bf):
        return _mla_prompt_attn(q, rows_bf, kv_lora)

    def swa_prompt(q, kv, x, sinks, w_o, b_o):
        return _swa_prompt_attn(q, kv, x, sinks, w_o, b_o, kv_heads, hd, swa_scale)

    y_p, rows_p, kv_p = _run_trunk(x_prompt.reshape(seq, d), jnp.arange(seq, dtype=jnp.int32), p, w, mla_q_scale,
                                   mla_prompt, swa_prompt)

    n_tok = dec_batch * dec_seq

    def mla_sample(l, q, rows, rows_bf):
        q_tok = jnp.transpose(q, (0, 2, 1, 3)).reshape(dec_batch, dec_seq, heads, e)
        q_seq = jnp.transpose(q_tok, (0, 2, 1, 3)).reshape(dec_batch, heads * dec_seq, e)
        o = _mla_sample_attn(q_seq, rows.reshape(dec_batch, dec_seq, e), cache_t, l, page_table, kv_lora)
        o = jnp.transpose(o.reshape(dec_batch, heads, dec_seq, kv_lora), (1, 0, 2, 3))
        return o.reshape(1, heads, n_tok, kv_lora)

    def swa_sample(q, kv, x, sinks, w_o, b_o):
        return _swa_sample_attn(q, kv, cache_swa_k, cache_swa_v, x, sinks, w_o, b_o, kv_heads, hd, dec_seq, swa_scale)

    pos_s = jnp.tile(past_len + jnp.arange(dec_seq, dtype=jnp.int32), dec_batch)
    y_s, rows_s, kv_s = _run_trunk(x_sample.reshape(n_tok, d), pos_s, p, w, mla_q_scale, mla_sample, swa_sample)

    kw = kv_heads * hd
    w_p = min(Q_BLOCK, seq)
    k_p = kv_p[seq - w_p:, :kw].reshape(1, w_p, kv_heads, hd)
    v_p = kv_p[seq - w_p:, kw:].reshape(1, w_p, kv_heads, hd)
    k_s = kv_s[:, :kw].reshape(dec_batch, dec_seq, kv_heads, hd)
    v_s = kv_s[:, kw:].reshape(dec_batch, dec_seq, kv_heads, hd)
    swa_k_sample = jnp.concatenate([cache_swa_k, k_s], axis=1)[:, dec_seq:]
    swa_v_sample = jnp.concatenate([cache_swa_v, v_s], axis=1)[:, dec_seq:]
    return (y_p.reshape(1, seq, d), y_s.reshape(dec_batch, dec_seq, d),
            rows_p.reshape(-1, 1, seq, e), rows_s.reshape(-1, dec_batch, dec_seq, e),
            k_p, v_p, swa_k_sample, swa_v_sample)
```

```python
import functools

import jax
import jax.numpy as jnp
from jax import lax
from jax.experimental import pallas as pl
from jax.experimental.pallas import tpu as pltpu

F32 = jnp.float32
BF16 = jnp.bfloat16

EPS = 1e-6
ROPE_THETA = 10000.0
FFN_RES_WEIGHT = 0.5
LANES = 128
MXU_DIM = 256
ROPE_DIM = 64
Q_BLOCK = 128
VMEM_LIMIT = 56 * 1024 * 1024
NEG_INF = float("-inf")
LOG2_E = 1.4426950408889634


def _cparams(n_axes, vmem=None):
    return pltpu.CompilerParams(dimension_semantics=("arbitrary",) * n_axes, vmem_limit_bytes=vmem)


def _const_spec(shape):
    nd = len(shape)
    return pl.BlockSpec(shape, lambda *_: (0,) * nd, pipeline_mode=pl.Buffered(1))


def _dot(a, b):
    return jnp.dot(a, b, preferred_element_type=F32)


def _dot_nt(a, b):
    return lax.dot_general(a, b, (((1,), (1,)), ((), ())), preferred_element_type=F32)


def _idiv(x, n):
    if n & (n - 1) == 0:
        return x >> (n.bit_length() - 1)
    return lax.div(x, jnp.full(x.shape, n, x.dtype))


def _irem(x, n):
    if n & (n - 1) == 0:
        return x & (n - 1)
    return lax.rem(x, jnp.full(x.shape, n, x.dtype))


def _rms(x, g):
    return x * lax.rsqrt(jnp.mean(x * x, axis=-1, keepdims=True) + EPS) * g


def _rope128(x, cos_t, sin_t):
    lane = lax.broadcasted_iota(jnp.int32, x.shape, 1)
    first_half = (lane & (ROPE_DIM // 2)) == 0
    partner = jnp.where(first_half, pltpu.roll(x, LANES - ROPE_DIM // 2, 1), pltpu.roll(x, ROPE_DIM // 2, 1))
    return x * cos_t + partner * sin_t


def _rope_tables(pos):
    half = ROPE_DIM // 2
    inv_freq = jnp.exp(-(jnp.arange(half, dtype=F32) / half) * jnp.log(F32(ROPE_THETA)))
    ang = pos.astype(F32)[:, None] * inv_freq[None, :]
    cos, sin = jnp.cos(ang), jnp.sin(ang)
    reps = LANES // ROPE_DIM
    cos_t = jnp.tile(cos, (1, 2 * reps))
    sin_t = jnp.tile(jnp.concatenate([-sin, sin], axis=1), (1, reps))
    return cos_t, sin_t


def _norm_linear_rope(x, g_ref, w_ref, b_ref, cos_ref, sin_ref, z_ref, rope_cols):
    z = _dot(_rms(x, g_ref[...]).astype(BF16), w_ref[...]) + b_ref[...]
    cos_t, sin_t = cos_ref[...], sin_ref[...]
    for c in range(rope_cols // LANES):
        z_ref[:, c * LANES:(c + 1) * LANES] = _rope128(z[:, c * LANES:(c + 1) * LANES], cos_t, sin_t)
    if rope_cols < z.shape[1]:
        z_ref[:, rope_cols:] = z[:, rope_cols:]


def _ffn_kernel(x_ref, g_ref, win_ref, wout_ref, *rest, d_ff, bounds, final, rope_cols):
    rest = list(rest)
    gf_ref = rest.pop(0) if final else None
    proj = [rest.pop(0) for _ in range(5)] if rope_cols is not None else None
    o_ref = rest.pop(0)
    x = x_ref[...]
    h = _rms(x, g_ref[...]).astype(BF16)
    acc = None
    for c0, c1 in zip(bounds[:-1], bounds[1:]):
        gate = _dot(h, win_ref[:, c0:c1])
        up = _dot(h, win_ref[:, d_ff + c0:d_ff + c1])
        act = (gate * (1.0 / (1.0 + jnp.exp(-gate))) * up).astype(BF16)
        part = _dot(act, wout_ref[c0:c1, :])
        acc = part if acc is None else acc + part
    y = x + FFN_RES_WEIGHT * acc
    if proj is not None:
        _norm_linear_rope(y, *proj, rest.pop(0), rope_cols)
    if final:
        y = _rms(y, gf_ref[...])
    o_ref[...] = y


def _ffn(x, g, w_in, w_out, half, final_g=None, proj=None, tm=512):
    m, d = x.shape
    d_ff = w_out.shape[2]
    pick = lambda r, c: pl.BlockSpec((None, None, r, c), lambda i: (*half, 0, 0), pipeline_mode=pl.Buffered(1))
    tm = min(tm, m)
    unit = MXU_DIM if d_ff % MXU_DIM == 0 else LANES
    n_units = d_ff // unit
    bounds = (0, (n_units + 1) // 2 * unit, d_ff) if n_units > 1 and d_ff % unit == 0 else (0, d_ff)
    in_specs = [
        pl.BlockSpec((tm, d), lambda i: (i, 0)),
        _const_spec((1, d)),
        pick(d, 2 * d_ff),
        pick(d_ff, d),
    ]
    args = [x, g.reshape(1, d), w_in, w_out]
    if final_g is not None:
        in_specs.append(_const_spec((1, d)))
        args.append(final_g.reshape(1, d))
    row_spec = lambda width: pl.BlockSpec((tm, width), lambda i: (i, 0))
    out_specs, out_shape, rope_cols = [row_spec(d)], [jax.ShapeDtypeStruct((m, d), F32)], None
    if proj is not None:
        g2, w2, b2, cos_t, sin_t, rope_cols = proj
        n = w2.shape[1]
        in_specs += [_const_spec((1, d)), _const_spec((d, n)), _const_spec((1, n)), row_spec(LANES), row_spec(LANES)]
        args += [g2.reshape(1, d), w2, b2.reshape(1, n), cos_t, sin_t]
        out_specs.append(row_spec(n))
        out_shape.append(jax.ShapeDtypeStruct((m, n), F32))
    out = pl.pallas_call(
        functools.partial(_ffn_kernel, d_ff=d_ff, bounds=bounds, final=final_g is not None, rope_cols=rope_cols),
        grid=(m // tm,),
        in_specs=in_specs,
        out_specs=out_specs,
        out_shape=out_shape,
        compiler_params=_cparams(1, VMEM_LIMIT),
        name="ffn_half",
    )(*args)
    return out if proj is not None else out[0]


def _mla_proj_kernel(x_ref, g_ref, wdown_ref, gq_ref, wuqn_ref, wuqr_ref, wuk_ref,
                     gkv_ref, cos_ref, sin_ref, q_ref, rows_ref, rowsb_ref, *, heads, nope, kv_lora, q_scale):
    tm = x_ref.shape[0]
    q_lora = gq_ref.shape[1]
    cos_t, sin_t = cos_ref[...], sin_ref[...]
    h = _rms(x_ref[...], g_ref[...]).astype(BF16)
    down = _dot(h, wdown_ref[...])
    cq = _rms(down[:, :q_lora], gq_ref[...]).astype(BF16)
    qn = _dot(cq, wuqn_ref[...])
    qr = _dot(cq, wuqr_ref[...])
    qr = jnp.concatenate(
        [_rope128(qr[:, c * LANES:(c + 1) * LANES], cos_t, sin_t) for c in range(qr.shape[1] // LANES)], axis=1)
    c_lat = _rms(down[:, q_lora:q_lora + kv_lora], gkv_ref[...])
    k_pe = _rope128(down[:, q_lora + kv_lora:], cos_t, sin_t)[:, :ROPE_DIM]
    rows_ref[:, :kv_lora] = c_lat
    rows_ref[:, kv_lora:] = k_pe
    rowsb_ref[:, :kv_lora] = c_lat.astype(BF16)
    rowsb_ref[:, kv_lora:] = k_pe.astype(BF16)
    for hd in range(heads):
        q_lat = (_dot(qn[:, hd * nope:(hd + 1) * nope].astype(BF16), wuk_ref[hd]) * q_scale).astype(BF16)
        q_pe = (qr[:, hd * ROPE_DIM:(hd + 1) * ROPE_DIM] * q_scale).astype(BF16)
        for r in range(tm // Q_BLOCK):
            q_ref[r, hd, :, :kv_lora] = q_lat[r * Q_BLOCK:(r + 1) * Q_BLOCK]
            q_ref[r, hd, :, kv_lora:] = q_pe[r * Q_BLOCK:(r + 1) * Q_BLOCK]


def _mla_proj(x, w, cos_t, sin_t, q_scale, tm=1024):
    m, d = x.shape
    heads, nope, kv_lora = w["w_uk"].shape
    e = kv_lora + ROPE_DIM
    q_lora = w["q_norm"].shape[0]
    assert q_lora % LANES == 0 and kv_lora % LANES == 0
    tm = min(tm, m)
    nb = tm // Q_BLOCK
    consts = [w["norm"].reshape(1, d), w["w_down"], w["q_norm"].reshape(1, q_lora), w["w_uq_n"], w["w_uq_r"],
              w["w_uk"], w["kv_norm"].reshape(1, kv_lora)]
    row_spec = lambda width: pl.BlockSpec((tm, width), lambda i: (i, 0))
    return pl.pallas_call(
        functools.partial(_mla_proj_kernel, heads=heads, nope=nope, kv_lora=kv_lora, q_scale=q_scale),
        grid=(m // tm,),
        in_specs=[row_spec(d)] + [_const_spec(c.shape) for c in consts] + [row_spec(LANES), row_spec(LANES)],
        out_specs=[pl.BlockSpec((nb, heads, Q_BLOCK, e), lambda i: (i, 0, 0, 0)), row_spec(e), row_spec(e)],
        out_shape=[jax.ShapeDtypeStruct((m // Q_BLOCK, heads, Q_BLOCK, e), BF16),
                   jax.ShapeDtypeStruct((m, e), F32),
                   jax.ShapeDtypeStruct((m, e), BF16)],
        compiler_params=_cparams(1),
        name="mla_proj",
    )(x, *consts, cos_t, sin_t)


def _flash_scores(q_ref, k_ref, s_scr, slot, start, *, tk):
    heads, _, e = q_ref.shape[1:]
    s_scr[slot] = _dot_nt(q_ref[0].reshape(heads * Q_BLOCK, e), k_ref[pl.ds(start, tk), :])


def _flash_update(k_ref, s_scr, slot, start, q_pos0, m_ref, l_ref, acc_ref, *, tk, kv_lora):
    s = s_scr[slot]
    if q_pos0 is not None:
        q_pos = q_pos0 + (lax.broadcasted_iota(jnp.int32, s.shape, 0) & (Q_BLOCK - 1))
        k_pos = start + lax.broadcasted_iota(jnp.int32, s.shape, 1)
        s = jnp.where(k_pos <= q_pos, s, NEG_INF)
    chunks = [s[:, j * LANES:(j + 1) * LANES] for j in range(tk // LANES)]
    m_prev = m_ref[...]
    m_new = jnp.maximum(m_prev, jnp.max(functools.reduce(jnp.maximum, chunks), axis=-1, keepdims=True))
    alpha = jnp.exp2(m_prev - m_new)
    ps = [jnp.exp2(ch - m_new) for ch in chunks]
    l_ref[...] = alpha * l_ref[...] + functools.reduce(jnp.add, ps)
    pv = _dot(jnp.concatenate(ps, axis=1).astype(BF16), k_ref[pl.ds(start, tk), :kv_lora])
    acc_ref[...] = acc_ref[...] * jnp.concatenate([alpha] * (kv_lora // LANES), axis=1) + pv
    m_ref[...] = m_new


def _mla_prompt_kernel(q_ref, k_ref, o_ref, s_scr, m_ref, l_ref, acc_ref, *, tk, kv_lora):
    i = pl.program_id(0)
    heads = q_ref.shape[1]
    m_ref[...] = jnp.full(m_ref.shape, NEG_INF, F32)
    l_ref[...] = jnp.zeros(l_ref.shape, F32)
    acc_ref[...] = jnp.zeros(acc_ref.shape, F32)
    scores = functools.partial(_flash_scores, q_ref, k_ref, s_scr, tk=tk)
    update = functools.partial(_flash_update, k_ref, s_scr, m_ref=m_ref, l_ref=l_ref, acc_ref=acc_ref,
                               tk=tk, kv_lora=kv_lora)
    at = lambda j: pl.multiple_of(j * tk, tk)
    n_full = (i * Q_BLOCK) // tk
    n_pairs = n_full // 2
    scores(0, at(0))

    def body(p, carry):
        j = 2 * p
        scores(1, at(j + 1))
        update(0, at(j), None)
        scores(0, at(j + 2))
        update(1, at(j + 1), None)
        return carry

    lax.fori_loop(0, n_pairs, body, 0)
    j0 = 2 * n_pairs
    q_pos0 = i * Q_BLOCK

    @pl.when(n_full == j0)
    def _():
        update(0, at(j0), q_pos0)

    @pl.when(n_full != j0)
    def _():
        scores(1, at(j0 + 1))
        update(0, at(j0), None)
        update(1, at(j0 + 1), q_pos0)

    l_tot = jnp.sum(l_ref[...], axis=-1, keepdims=True)
    o_ref[0] = (acc_ref[...] / l_tot).astype(o_ref.dtype).reshape(heads, Q_BLOCK, kv_lora)


def _mla_prompt_attn(q, rows_bf, kv_lora, tk=1024):
    nb, heads, _, e = q.shape
    s = rows_bf.shape[0]
    tk = min(tk, s)
    assert s % tk == 0 and tk % Q_BLOCK == 0
    rows = heads * Q_BLOCK
    return pl.pallas_call(
        functools.partial(_mla_prompt_kernel, tk=tk, kv_lora=kv_lora),
        grid=(nb,),
        in_specs=[pl.BlockSpec((1, heads, Q_BLOCK, e), lambda i: (i, 0, 0, 0)), _const_spec((s, e))],
        out_specs=pl.BlockSpec((1, heads, Q_BLOCK, kv_lora), lambda i: (i, 0, 0, 0)),
        out_shape=jax.ShapeDtypeStruct((nb, heads, Q_BLOCK, kv_lora), BF16),
        scratch_shapes=[pltpu.VMEM((2, rows, tk), F32), pltpu.VMEM((rows, LANES), F32),
                        pltpu.VMEM((rows, LANES), F32), pltpu.VMEM((rows, kv_lora), F32)],
        compiler_params=_cparams(1, VMEM_LIMIT),
        name="mla_prompt_attn",
    )(q, rows_bf)


def _mla_sample_kernel(pt_ref, q_ref, new_ref, cache_ref, o_ref, pbuf, sem, kbuf_ref, nbuf_ref, m_ref, l_ref, acc_ref,
                       *, layer, n_group, dec_seq, kv_lora):
    p_idx = pl.program_id(1)
    n_steps = pl.num_programs(0) * pl.num_programs(1)
    t = pl.program_id(0) * pl.num_programs(1) + p_idx
    n_slots = pbuf.shape[0]
    ahead = n_slots - 1
    slot = lax.rem(t, n_slots)
    page = pbuf.shape[3]

    def page_copies(step, sl):
        return [pltpu.make_async_copy(cache_ref.at[layer, pt_ref[step * n_group + g]], pbuf.at[sl, g], sem.at[sl])
                for g in range(n_group)]

    @pl.when(t == 0)
    def _():
        for step in range(ahead):
            @pl.when(step < n_steps)
            def _():
                for c in page_copies(step, step):
                    c.start()

    @pl.when(t + ahead < n_steps)
    def _():
        for c in page_copies(t + ahead, lax.rem(t + ahead, n_slots)):
            c.start()

    q = q_ref[...]

    @pl.when(p_idx == 0)
    def _():
        nbuf_ref[...] = jnp.zeros(nbuf_ref.shape, F32)
        nbuf_ref[:dec_seq, :] = new_ref[...]
        nk = nbuf_ref[...].astype(BF16)
        s = _dot_nt(q, nk)
        tok = _irem(lax.broadcasted_iota(jnp.int32, s.shape, 0), dec_seq)
        s = jnp.where(lax.broadcasted_iota(jnp.int32, s.shape, 1) <= tok, s, NEG_INF)
        m0 = jnp.max(s, axis=-1, keepdims=True)
        p0 = jnp.exp2(s - m0)
        m_ref[...] = m0
        l_ref[...] = jnp.sum(p0, axis=-1, keepdims=True)
        acc_ref[...] = _dot(p0.astype(BF16), nk[:, :kv_lora])

    for c in page_copies(t, slot):
        c.wait()
    for g in range(n_group):
        kbuf_ref[:, g * page:(g + 1) * page] = pbuf[slot, g].astype(BF16)
    kt = kbuf_ref[...]
    s = _dot(q, kt)
    m_prev = m_ref[...]
    m_new = jnp.maximum(m_prev, jnp.max(s, axis=-1, keepdims=True))
    alpha = jnp.exp2(m_prev - m_new)
    p = jnp.exp2(s - m_new)
    l_ref[...] = alpha * l_ref[...] + jnp.sum(p, axis=-1, keepdims=True)
    acc_ref[...] = alpha * acc_ref[...] + _dot_nt(p.astype(BF16), kt[:kv_lora, :])
    m_ref[...] = m_new

    @pl.when(p_idx == pl.num_programs(1) - 1)
    def _():
        o_ref[...] = (acc_ref[...] / l_ref[...]).astype(o_ref.dtype)


def _mla_sample_attn(q, new_rows, cache_t, layer, page_table, kv_lora, n_group=64, n_slots=3):
    b, qrows, e = q.shape
    dec_seq = new_rows.shape[1]
    n_pages = page_table.shape[1]
    page = cache_t.shape[3]
    n_group = min(n_group, n_pages)
    assert n_pages % n_group == 0 and dec_seq <= page

    grid_spec = pltpu.PrefetchScalarGridSpec(
        num_scalar_prefetch=1,
        grid=(b, n_pages // n_group),
        in_specs=[pl.BlockSpec((None, qrows, e), lambda bi, pi, pt: (bi, 0, 0)),
                  pl.BlockSpec((None, dec_seq, e), lambda bi, pi, pt: (bi, 0, 0)),
                  pl.BlockSpec(memory_space=pl.ANY)],
        out_specs=pl.BlockSpec((None, qrows, kv_lora), lambda bi, pi, pt: (bi, 0, 0)),
        scratch_shapes=[pltpu.VMEM((n_slots, n_group, e, page), F32), pltpu.SemaphoreType.DMA((n_slots,)),
                        pltpu.VMEM((e, n_group * page), BF16), pltpu.VMEM((page, e), F32),
                        pltpu.VMEM((qrows, 1), F32), pltpu.VMEM((qrows, 1), F32), pltpu.VMEM((qrows, kv_lora), F32)],
    )
    return pl.pallas_call(
        functools.partial(_mla_sample_kernel, layer=layer, n_group=n_group, dec_seq=dec_seq, kv_lora=kv_lora),
        grid_spec=grid_spec,
        out_shape=jax.ShapeDtypeStruct((b, qrows, kv_lora), BF16),
        compiler_params=_cparams(2),
        name="mla_sample_attn",
    )(page_table.reshape(-1), q, new_rows, cache_t)


def _mla_out_kernel(o_ref, x_ref, wuv_ref, wo_ref, y_ref, ocat_scr):
    nblk, heads, tq, kv_lora = o_ref.shape
    v_dim = wuv_ref.shape[2]
    for hd in range(heads):
        o_h = _dot(o_ref[:, hd].reshape(nblk * tq, kv_lora), wuv_ref[hd])
        ocat_scr[:, hd * v_dim:(hd + 1) * v_dim] = o_h.astype(BF16)
    y_ref[...] = x_ref[...] + _dot(ocat_scr[...], wo_ref[...])


def _mla_out(o_lat, x, w_uv, w_o, rows_per_step=512):
    nb, heads, tq, kv_lora = o_lat.shape
    d = x.shape[1]
    assert o_lat.dtype == BF16
    nblk = max(1, min(nb, rows_per_step // tq))
    assert nb % nblk == 0
    return pl.pallas_call(
        _mla_out_kernel,
        grid=(nb // nblk,),
        in_specs=[pl.BlockSpec((nblk, heads, tq, kv_lora), lambda i: (i, 0, 0, 0)),
                  pl.BlockSpec((nblk * tq, d), lambda i: (i, 0)),
                  _const_spec(w_uv.shape), _const_spec(w_o.shape)],
        out_specs=pl.BlockSpec((nblk * tq, d), lambda i: (i, 0)),
        out_shape=jax.ShapeDtypeStruct(x.shape, F32),
        scratch_shapes=[pltpu.VMEM((nblk * tq, w_o.shape[0]), BF16)],
        compiler_params=_cparams(1),
        name="mla_out",
    )(o_lat, x, w_uv, w_o)


def _norm_linear_kernel(x_ref, g_ref, w_ref, b_ref, cos_ref, sin_ref, y_ref, *, rope_cols):
    _norm_linear_rope(x_ref[...], g_ref, w_ref, b_ref, cos_ref, sin_ref, y_ref, rope_cols)


def _norm_linear(x, g, w, b, cos_t, sin_t, rope_cols, tm=512):
    m, d = x.shape
    n = w.shape[1]
    tm = min(tm, m)
    row_spec = lambda width: pl.BlockSpec((tm, width), lambda i: (i, 0))
    return pl.pallas_call(
        functools.partial(_norm_linear_kernel, rope_cols=rope_cols),
        grid=(m // tm,),
        in_specs=[row_spec(d), _const_spec((1, d)), _const_spec((d, n)), _const_spec((1, n)),
                  row_spec(LANES), row_spec(LANES)],
        out_specs=row_spec(n),
        out_shape=jax.ShapeDtypeStruct((m, n), F32),
        compiler_params=_cparams(1),
        name="norm_linear_rope",
    )(x, g.reshape(1, d), w, b.reshape(1, n), cos_t, sin_t)


def _sink_attend(s, valid, sink, v):
    s = jnp.where(valid, s, NEG_INF)
    m = jnp.maximum(jnp.max(s, axis=-1, keepdims=True), sink)
    p = jnp.exp(s - m)
    pr = p / (jnp.sum(p, axis=-1, keepdims=True) + jnp.exp(sink - m))
    return _dot(pr.astype(BF16), v)


def _stack_group(q, kvh, group, hd):
    return jnp.concatenate([q[:, (kvh * group + g) * hd:(kvh * group + g + 1) * hd] for g in range(group)], axis=0)


def _sink_rows(sinks_ref, kvh, group, t):
    row = lax.broadcasted_iota(jnp.int32, (group * t, 1), 0)
    sink = jnp.full((group * t, 1), sinks_ref[kvh * group], F32)
    for g in range(1, group):
        sink = jnp.where(row >= g * t, sinks_ref[kvh * group + g], sink)
    return sink


def _swa_finish(o_scr, x_ref, wo_ref, bo_ref, y_ref):
    y_ref[...] = x_ref[...] + _dot(o_scr[...].astype(BF16), wo_ref[...]) + bo_ref[...]


def _swa_prompt_kernel(sinks_ref, q_ref, cur_ref, pk_ref, pv_ref, x_ref, wo_ref, bo_ref, y_ref, o_scr, s_scr, p_scr,
                       *, kv_heads, group, hd, scale):
    i = pl.program_id(0)
    t = Q_BLOCK
    nb = q_ref.shape[0] // t
    kw = kv_heads * hd
    shape = (group * t, 2 * t)
    r = _irem(lax.broadcasted_iota(jnp.int32, shape, 0), t)
    j = lax.broadcasted_iota(jnp.int32, shape, 1)
    band = (j > r) & (j <= r + t)
    band_first = band & ((j >= t) | (i > 0))
    kks, vvs = [], []
    for b in range(nb):
        r0, r1 = b * t, (b + 1) * t
        pk = pk_ref[...] if b == 0 else cur_ref[r0 - t:r0, :kw]
        pv = pv_ref[...] if b == 0 else cur_ref[r0 - t:r0, kw:]
        kks.append(jnp.concatenate([pk, cur_ref[r0:r1, :kw]], axis=0).astype(BF16))
        vvs.append(jnp.concatenate([pv, cur_ref[r0:r1, kw:]], axis=0).astype(BF16))
    for b in range(nb):
        q = q_ref[b * t:(b + 1) * t, :]
        for kvh in range(kv_heads):
            qg = _stack_group(q, kvh, group, hd).astype(BF16)
            s_scr[b * kv_heads + kvh] = _dot_nt(qg, kks[b][:, kvh * hd:(kvh + 1) * hd])
    for b in range(nb):
        for kvh in range(kv_heads):
            s = jnp.where(band_first if b == 0 else band, s_scr[b * kv_heads + kvh] * scale, NEG_INF)
            sink = _sink_rows(sinks_ref, kvh, group, t)
            m = jnp.maximum(jnp.max(s, axis=-1, keepdims=True), sink)
            p = jnp.exp(s - m)
            pr = p / (jnp.sum(p, axis=-1, keepdims=True) + jnp.exp(sink - m))
            p_scr[b * kv_heads + kvh] = pr.astype(BF16)
    for b in range(nb):
        for kvh in range(kv_heads):
            o = _dot(p_scr[b * kv_heads + kvh], vvs[b][:, kvh * hd:(kvh + 1) * hd])
            for g in range(group):
                hh = kvh * group + g
                o_scr[b * t:(b + 1) * t, hh * hd:(hh + 1) * hd] = o[g * t:(g + 1) * t]
    _swa_finish(o_scr, x_ref, wo_ref, bo_ref, y_ref)


def _swa_prompt_attn(q, kv, x, sinks, w_o, b_o, kv_heads, hd, scale, rows_per_step=512):
    m, d = x.shape
    qw = q.shape[1]
    kw = kv_heads * hd
    t = Q_BLOCK
    group = qw // kw
    tm = min(rows_per_step, m)
    nb = tm // t
    assert m % tm == 0 and tm % t == 0
    prev = lambda col: pl.BlockSpec((t, kw), lambda i: (jnp.maximum(i * nb - 1, 0), col))
    return pl.pallas_call(
        functools.partial(_swa_prompt_kernel, kv_heads=kv_heads, group=group, hd=hd, scale=scale),
        grid=(m // tm,),
        in_specs=[pl.BlockSpec(memory_space=pltpu.SMEM),
                  pl.BlockSpec((tm, qw), lambda i: (i, 0)),
                  pl.BlockSpec((tm, 2 * kw), lambda i: (i, 0)),
                  prev(0), prev(1),
                  pl.BlockSpec((tm, d), lambda i: (i, 0)),
                  _const_spec(w_o.shape), _const_spec((1, d))],
        out_specs=pl.BlockSpec((tm, d), lambda i: (i, 0)),
        out_shape=jax.ShapeDtypeStruct((m, d), F32),
        scratch_shapes=[pltpu.VMEM((tm, qw), F32), pltpu.VMEM((nb * kv_heads, group * t, 2 * t), F32),
                        pltpu.VMEM((nb * kv_heads, group * t, 2 * t), BF16)],
        compiler_params=_cparams(1),
        name="swa_prompt_attn",
    )(sinks, q, kv, kv, kv, x, w_o, b_o.reshape(1, d))


def _swa_sample_kernel(sinks_ref, q_ref, new_ref, ck_ref, cv_ref, x_ref, wo_ref, bo_ref, y_ref, o_scr,
                       *, kv_heads, group, hd, dec_seq, scale):
    n_seq, wb, kw = ck_ref.shape
    t = n_seq * dec_seq
    q = q_ref[...]
    kk = jnp.concatenate([ck_ref[...].reshape(n_seq * wb, kw), new_ref[:, :kw]], axis=0).astype(BF16)
    vv = jnp.concatenate([cv_ref[...].reshape(n_seq * wb, kw), new_ref[:, kw:]], axis=0).astype(BF16)
    n_old = n_seq * wb
    shape = (group * t, n_old + t)
    r = _irem(lax.broadcasted_iota(jnp.int32, shape, 0), t)
    r_seq, r_tok = _idiv(r, dec_seq), _irem(r, dec_seq)
    j = lax.broadcasted_iota(jnp.int32, shape, 1)
    old = j < n_old
    j_new = jnp.maximum(j - n_old, 0)
    j_seq = jnp.where(old, _idiv(j, wb), _idiv(j_new, dec_seq))
    j_pos = jnp.where(old, _irem(j, wb), wb + _irem(j_new, dec_seq))
    valid = (j_seq == r_seq) & (j_pos > r_tok) & (j_pos <= r_tok + wb)
    for kvh in range(kv_heads):
        qg = _stack_group(q, kvh, group, hd).astype(BF16)
        s = _dot_nt(qg, kk[:, kvh * hd:(kvh + 1) * hd]) * scale
        o = _sink_attend(s, valid, _sink_rows(sinks_ref, kvh, group, t), vv[:, kvh * hd:(kvh + 1) * hd])
        for g in range(group):
            hh = kvh * group + g
            o_scr[:, hh * hd:(hh + 1) * hd] = o[g * t:(g + 1) * t]
    _swa_finish(o_scr, x_ref, wo_ref, bo_ref, y_ref)


def _swa_sample_attn(q, kv_new, cache_k, cache_v, x, sinks, w_o, b_o, kv_heads, hd, dec_seq, scale, n_seq=8):
    m, d = x.shape
    qw = q.shape[1]
    kw = kv_heads * hd
    b, wb = cache_k.shape[:2]
    n_seq = min(n_seq, b)
    t = n_seq * dec_seq
    group = qw // kw
    return pl.pallas_call(
        functools.partial(_swa_sample_kernel, kv_heads=kv_heads, group=group, hd=hd, dec_seq=dec_seq, scale=scale),
        grid=(b // n_seq,),
        in_specs=[pl.BlockSpec(memory_space=pltpu.SMEM),
                  pl.BlockSpec((t, qw), lambda i: (i, 0)),
                  pl.BlockSpec((t, 2 * kw), lambda i: (i, 0)),
                  pl.BlockSpec((n_seq, wb, kw), lambda i: (i, 0, 0)),
                  pl.BlockSpec((n_seq, wb, kw), lambda i: (i, 0, 0)),
                  pl.BlockSpec((t, d), lambda i: (i, 0)),
                  _const_spec(w_o.shape), _const_spec((1, d))],
        out_specs=pl.BlockSpec((t, d), lambda i: (i, 0)),
        out_shape=jax.ShapeDtypeStruct((m, d), F32),
        scratch_shapes=[pltpu.VMEM((t, qw), F32)],
        compiler_params=_cparams(1),
        name="swa_sample_attn",
    )(sinks, q, kv_new, cache_k.reshape(b, wb, kw), cache_v.reshape(b, wb, kw), x, w_o, b_o.reshape(1, d))


def _prep_weights(p):
    n_a, kv_lora, heads, nope = p["mla_w_uk"].shape
    q_lora = p["mla_w_dq"].shape[2]
    w = {"ffn_in": p["ffn_w_in"].astype(BF16), "ffn_out": p["ffn_w_out"].astype(BF16), "mla": []}
    for l in range(n_a):
        w_uq = p["mla_w_uq"][l].reshape(q_lora, heads, nope + ROPE_DIM)
        w_dkv = p["mla_w_dkv"][l]
        w["mla"].append({
            "norm": p["mla_norm"][l], "q_norm": p["mla_q_norm"][l], "kv_norm": p["mla_kv_norm"][l],
            "w_down": jnp.concatenate(
                [p["mla_w_dq"][l], jnp.pad(w_dkv, ((0, 0), (0, LANES - ROPE_DIM)))], axis=1).astype(BF16),
            "w_uq_n": w_uq[:, :, :nope].reshape(q_lora, heads * nope).astype(BF16),
            "w_uq_r": w_uq[:, :, nope:].reshape(q_lora, heads * ROPE_DIM).astype(BF16),
            "w_uk": jnp.transpose(p["mla_w_uk"][l], (1, 2, 0)).astype(BF16),
            "w_uv": jnp.transpose(p["mla_w_uv"][l], (1, 0, 2)).astype(BF16),
            "w_o": p["mla_w_o"][l].astype(BF16),
        })
    w["swa_kv"] = p["swa_w_kv"].astype(BF16)
    w["swa_q"] = p["swa_w_q"].astype(BF16)
    w["swa_o"] = p["swa_w_o"].astype(BF16)
    return w


def _run_trunk(x, pos, p, w, mla_q_scale, mla_attend, swa_attend):
    depth = p["ffn_norm"].shape[0]
    n_a = p["mla_norm"].shape[0]
    kv_w = w["swa_kv"].shape[1]
    cos_t, sin_t = _rope_tables(pos)
    rows_all, kv = [], None
    kv_proj = (p["kv_norm"], w["swa_kv"], p["swa_b_kv"], cos_t, sin_t, kv_w // 2)
    for l in range(depth):
        if l == n_a and kv is None:
            kv = _norm_linear(x, *kv_proj[:5], rope_cols=kv_proj[5])
        if l < n_a:
            x = _ffn(x, p["ffn_norm"][l, 0], w["ffn_in"], w["ffn_out"], (l, 0))
            wl = w["mla"][l]
            q, rows, rows_bf = _mla_proj(x, wl, cos_t, sin_t, mla_q_scale)
            o_lat = mla_attend(l, q, rows, rows_bf)
            x = _mla_out(o_lat, x, wl["w_uv"], wl["w_o"])
            rows_all.append(rows)
        else:
            jl = l - n_a
            q_proj = (p["swa_norm"][jl], w["swa_q"][jl], p["swa_b_q"][jl], cos_t, sin_t, w["swa_q"].shape[2])
            x, q = _ffn(x, p["ffn_norm"][l, 0], w["ffn_in"], w["ffn_out"], (l, 0), proj=q_proj)
            x = swa_attend(q, kv, x, p["swa_sinks"][jl], w["swa_o"][jl], p["swa_b_o"][jl])
        last = l == depth - 1
        out = _ffn(x, p["ffn_norm"][l, 1], w["ffn_in"], w["ffn_out"], (l, 1),
                   final_g=p["final_norm"] if last else None, proj=kv_proj if l + 1 == n_a and not last else None)
        x, kv = out if isinstance(out, (list, tuple)) else (out, kv)
    return x, jnp.stack(rows_all, axis=0), kv


def kernel(x_prompt, x_sample, cache_mla, cache_swa_k, cache_swa_v, page_table, ffn_norm, ffn_w_in, ffn_w_out, mla_norm, mla_w_dq, mla_q_norm, mla_w_uq, mla_w_dkv, mla_kv_norm, mla_w_uk, mla_w_uv, mla_w_o, kv_norm, swa_w_kv, swa_b_kv, swa_norm, swa_w_q, swa_b_q, swa_sinks, swa_w_o, swa_b_o, final_norm):
    p = {
        "ffn_norm": ffn_norm, "ffn_w_in": ffn_w_in, "ffn_w_out": ffn_w_out,
        "mla_norm": mla_norm, "mla_w_dq": mla_w_dq, "mla_q_norm": mla_q_norm, "mla_w_uq": mla_w_uq,
        "mla_w_dkv": mla_w_dkv, "mla_kv_norm": mla_kv_norm, "mla_w_uk": mla_w_uk, "mla_w_uv": mla_w_uv,
        "mla_w_o": mla_w_o,
        "kv_norm": kv_norm, "swa_w_kv": swa_w_kv, "swa_b_kv": swa_b_kv,
        "swa_norm": swa_norm, "swa_w_q": swa_w_q, "swa_b_q": swa_b_q, "swa_sinks": swa_sinks,
        "swa_w_o": swa_w_o, "swa_b_o": swa_b_o, "final_norm": final_norm,
    }
    batch, seq, d = x_prompt.shape
    dec_batch, dec_seq, _ = x_sample.shape
    _, kv_lora, heads, nope = mla_w_uk.shape
    e = kv_lora + ROPE_DIM
    kv_heads, hd = cache_swa_k.shape[2:]
    past_len = page_table.shape[1] * cache_mla.shape[2]
    mla_q_scale = float(nope + ROPE_DIM) ** -0.5 * LOG2_E
    swa_scale = float(hd) ** -0.5
    cache_t = jnp.swapaxes(cache_mla, 2, 3)
    assert batch == 1 and seq % Q_BLOCK == 0 and cache_swa_k.shape[1] == Q_BLOCK
    w = _prep_weights(p)

    def mla_prompt(l, q, rows, rows_bf):
        return _mla_prompt_attn(q, rows_bf, kv_lora)

    def swa_prompt(q, kv, x, sinks, w_o, b_o):
        return _swa_prompt_attn(q, kv, x, sinks, w_o, b_o, kv_heads, hd, swa_scale)

    y_p, rows_p, kv_p = _run_trunk(x_prompt.reshape(seq, d), jnp.arange(seq, dtype=jnp.int32), p, w, mla_q_scale,
                                   mla_prompt, swa_prompt)

    n_tok = dec_batch * dec_seq

    def mla_sample(l, q, rows, rows_bf):
        q_tok = jnp.transpose(q, (0, 2, 1, 3)).reshape(dec_batch, dec_seq, heads, e)
        q_seq = jnp.transpose(q_tok, (0, 2, 1, 3)).reshape(dec_batch, heads * dec_seq, e)
        o = _mla_sample_attn(q_seq, rows.reshape(dec_batch, dec_seq, e), cache_t, l, page_table, kv_lora)
        o = jnp.transpose(o.reshape(dec_batch, heads, dec_seq, kv_lora), (1, 0, 2, 3))
        return o.reshape(1, heads, n_tok, kv_lora)

    def swa_sample(q, kv, x, sinks, w_o, b_o):
        return _swa_sample_attn(q, kv, cache_swa_k, cache_swa_v, x, sinks, w_o, b_o, kv_heads, hd, dec_seq, swa_scale)

    pos_s = jnp.tile(past_len + jnp.arange(dec_seq, dtype=jnp.int32), dec_batch)
    y_s, rows_s, kv_s = _run_trunk(x_sample.reshape(n_tok, d), pos_s, p, w, mla_q_scale, mla_sample, swa_sample)

    kw = kv_heads * hd
    w_p = min(Q_BLOCK, seq)
    k_p = kv_p[seq - w_p:, :kw].reshape(1, w_p, kv_heads, hd)
    v_p = kv_p[seq - w_p:, kw:].reshape(1, w_p, kv_heads, hd)
    k_s = kv_s[:, :kw].reshape(dec_batch, dec_seq, kv_heads, hd)
    v_s = kv_s[:, kw:].reshape(dec_batch, dec_seq, kv_heads, hd)
    swa_k_sample = jnp.concatenate([cache_swa_k, k_s], axis=1)[:, dec_seq:]
    swa_v_sample = jnp.concatenate([cache_swa_v, v_s], axis=1)[:, dec_seq:]
    return (y_p.reshape(1, seq, d), y_s.reshape(dec_batch, dec_seq, d),
            rows_p.reshape(-1, 1, seq, e), rows_s.reshape(-1, dec_batch, dec_seq, e),
            k_p, v_p, swa_k_sample, swa_v_sample)
```
